```python
import math
import jax, jax.numpy as jnp
from jax import lax
import numpy as np

D_MODEL = 1024
BATCH = 2
SEQ = 8192
DEPTH = 4
DEC_BATCH = 128
DEC_SEQ = 1
PAST_LEN = 8192
PAGE_SIZE = 128

N_MIXERS = 3
N_SB_LAYERS = (DEPTH + 2) // 3
N_S5_LAYERS = (DEPTH + 1) // 3
N_MLA_LAYERS = DEPTH // 3
D_FF = 4 * D_MODEL
SB_HEADS = 16
SB_KV_HEADS = 4
SB_HEAD_DIM = D_MODEL // SB_HEADS
SB_GROUP = SB_HEADS // SB_KV_HEADS
SB_SCALE = SB_HEAD_DIM ** -0.5
S5_GROUP_CH = 16
S5_GROUPS = D_MODEL // S5_GROUP_CH
S5_STATE = 64
S5_DT_MIN = 1e-3
S5_DT_MAX = 1e-1
MLA_HEADS = 16
MLA_Q_RANK = 384
MLA_KV_RANK = 256
MLA_NOPE = 64
MLA_ROPE = 32
MLA_V = 64
MLA_QK = MLA_NOPE + MLA_ROPE
MLA_SCALE = MLA_QK ** -0.5
ROPE_THETA = 10000.0
Q_BLOCK = 128
RMS_EPS = 1e-6
NEG_INF = -1e30

kernel_name = 'hybrid_sb_s5_mla_decoder_step'


def rms_norm(x, g):
    xf = x.astype(jnp.float32)
    y = xf * lax.rsqrt(jnp.mean(xf * xf, axis=-1, keepdims=True) + RMS_EPS)
    return (y * g.astype(jnp.float32)).astype(x.dtype)


def rope(x, pos):
    half = x.shape[-1] // 2
    freqs = ROPE_THETA ** (-jnp.arange(half, dtype=jnp.float32) / half)
    ang = pos.astype(jnp.float32)[:, None] * freqs[None, :]
    shape = (1, pos.shape[0]) + (1,) * (x.ndim - 3) + (half,)
    cos = jnp.cos(ang).reshape(shape)
    sin = jnp.sin(ang).reshape(shape)
    xf = x.astype(jnp.float32)
    x1, x2 = xf[..., :half], xf[..., half:]
    return jnp.concatenate([x1 * cos - x2 * sin, x2 * cos + x1 * sin], axis=-1).astype(x.dtype)


def sweep_query_blocks(fn, qs, q_pos):
    t = q_pos.shape[0]
    blk = min(Q_BLOCK, t)
    nb = -(-t // blk)
    pad = nb * blk - t

    def to_blocks(a):
        a = jnp.pad(a, [(0, 0), (0, pad)] + [(0, 0)] * (a.ndim - 2))
        return jnp.moveaxis(a.reshape((a.shape[0], nb, blk) + a.shape[2:]), 1, 0)

    pos_b = jnp.pad(q_pos, (0, pad), constant_values=-1).reshape(nb, blk)
    out = lax.map(lambda args: fn(args[0], args[1]), (tuple(to_blocks(a) for a in qs), pos_b))
    out = jnp.moveaxis(out, 0, 1)
    out = out.reshape((out.shape[0], nb * blk) + out.shape[3:])
    return out[:, :t]


def sb_attend(q, q_pos, k, v, k_pos, k_past, v_past):
    qf = q.astype(jnp.float32) * SB_SCALE
    z = jnp.einsum('btkgd,bskd->bkgts', qf, k)
    valid = k_pos[None, :] < q_pos[:, None]
    lneg = jnp.where(valid, jax.nn.log_sigmoid(-z), 0.0)
    tail = lax.cumsum(lneg, axis=z.ndim - 1, reverse=True) - lneg
    w = jnp.where(valid, jnp.exp(jax.nn.log_sigmoid(z) + tail), 0.0)
    o = jnp.einsum('bkgts,bskd->btkgd', w, v)
    if k_past is not None:
        zp = jnp.einsum('btkgd,bskd->bkgts', qf, k_past)
        lp = jax.nn.log_sigmoid(-zp)
        tail_p = lax.cumsum(lp, axis=zp.ndim - 1, reverse=True) - lp + jnp.sum(lneg, axis=-1, keepdims=True)
        o = o + jnp.einsum('bkgts,bskd->btkgd', jnp.exp(jax.nn.log_sigmoid(zp) + tail_p), v_past)
    return o


def sb_mixer(h, pos, w_qkv, g_q, g_k, w_o, k_past=None, v_past=None):
    b, t, _ = h.shape
    nq = SB_HEADS * SB_HEAD_DIM
    nkv = SB_KV_HEADS * SB_HEAD_DIM
    qkv = h @ w_qkv
    q = rms_norm(qkv[..., :nq].reshape(b, t, SB_KV_HEADS, SB_GROUP, SB_HEAD_DIM), g_q)
    k = rms_norm(qkv[..., nq:nq + nkv].reshape(b, t, SB_KV_HEADS, SB_HEAD_DIM), g_k)
    v = qkv[..., nq + nkv:].reshape(b, t, SB_KV_HEADS, SB_HEAD_DIM)
    o = sweep_query_blocks(lambda qs, pb: sb_attend(qs[0], pb, k, v, pos, k_past, v_past), (q,), pos)
    out = o.reshape(b, t, nq).astype(h.dtype) @ w_o
    return out.astype(h.dtype), k, v


def s5_mixer(h, a_re, a_im, log_dt, b_re, b_im, c_re, c_im, d, w_glu, h0_re, h0_im):
    bsz, t, _ = h.shape
    f32 = jnp.float32
    u = h.astype(f32).reshape(bsz, t, S5_GROUPS, S5_GROUP_CH)
    a_re = a_re.astype(f32)
    a_im = a_im.astype(f32)
    dt = jnp.exp(log_dt.astype(f32))[:, None]
    mag = jnp.exp(dt * a_re)
    abar_re = mag * jnp.cos(dt * a_im)
    abar_im = mag * jnp.sin(dt * a_im)
    den = a_re * a_re + a_im * a_im
    xr, yi = abar_re - 1.0, abar_im
    coef_re = (xr * a_re + yi * a_im) / den
    coef_im = (yi * a_re - xr * a_im) / den
    b_re = b_re.astype(f32)
    b_im = b_im.astype(f32)
    bbar_re = coef_re[..., None] * b_re - coef_im[..., None] * b_im
    bbar_im = coef_re[..., None] * b_im + coef_im[..., None] * b_re
    bu_re = jnp.einsum('btgc,gpc->btgp', u, bbar_re)
    bu_im = jnp.einsum('btgc,gpc->btgp', u, bbar_im)
    h0_re = h0_re.astype(f32)
    h0_im = h0_im.astype(f32)
    bu_re = bu_re.at[:, 0].add(abar_re * h0_re - abar_im * h0_im)
    bu_im = bu_im.at[:, 0].add(abar_re * h0_im + abar_im * h0_re)
    ar_full = jnp.broadcast_to(abar_re, bu_re.shape)
    ai_full = jnp.broadcast_to(abar_im, bu_re.shape)

    def combine(e1, e2):
        a1r, a1i, b1r, b1i = e1
        a2r, a2i, b2r, b2i = e2
        return (a2r * a1r - a2i * a1i, a2r * a1i + a2i * a1r,
                a2r * b1r - a2i * b1i + b2r, a2r * b1i + a2i * b1r + b2i)

    _, _, st_re, st_im = lax.associative_scan(combine, (ar_full, ai_full, bu_re, bu_im), axis=1)
    y = jnp.einsum('gcp,btgp->btgc', c_re.astype(f32), st_re) - jnp.einsum('gcp,btgp->btgc', c_im.astype(f32), st_im)
    y = y.reshape(bsz, t, D_MODEL) + d.astype(f32) * u.reshape(bsz, t, D_MODEL)
    y = jax.nn.gelu(y)
    vg = y @ w_glu.astype(f32)
    out = vg[..., :D_MODEL] * jax.nn.sigmoid(vg[..., D_MODEL:])
    return out.astype(h.dtype), st_re[:, -1], st_im[:, -1]


def mla_attend(qs, q_pos, segments):
    qa = qs[0].astype(jnp.float32)
    qp = qs[1].astype(jnp.float32)
    logits = []
    for lat, kpe, kinv, k_pos in segments:
        s = (jnp.einsum('bthl,bsl->bhts', qa, lat) + jnp.einsum('bthr,bsr->bhts', qp, kpe)) * MLA_SCALE
        s = s * jnp.swapaxes(kinv, 1, 2)[:, :, None, :]
        if k_pos is not None:
            s = jnp.where(k_pos[None, :] <= q_pos[:, None], s, NEG_INF)
        logits.append(s)
    m = logits[0].max(axis=-1, keepdims=True)
    for s in logits[1:]:
        m = jnp.maximum(m, s.max(axis=-1, keepdims=True))
    den = 0.0
    ctx = 0.0
    for s, seg in zip(logits, segments):
        e = jnp.exp(s - m)
        den = den + jnp.sum(e, axis=-1, keepdims=True)
        ctx = ctx + jnp.einsum('bhts,bsl->bthl', e, seg[0])
    return ctx / jnp.swapaxes(den, 1, 2)


def mla_mixer(h, pos, w_dq, g_ql, w_uq, w_dkv, g_kvl, w_uk, w_uv, g_q, g_k, w_o, past=None):
    b, t, _ = h.shape
    f32 = jnp.float32
    cq = rms_norm(h @ w_dq, g_ql)
    q = rms_norm((cq @ w_uq).reshape(b, t, MLA_HEADS, MLA_QK), g_q)
    q_pe = rope(q[..., MLA_NOPE:], pos)
    q_abs = jnp.einsum('bthn,lhn->bthl', q[..., :MLA_NOPE] * g_k[:MLA_NOPE], w_uk)
    ckv = h @ w_dkv
    lat = rms_norm(ckv[..., :MLA_KV_RANK], g_kvl)
    kpe_raw = ckv[..., MLA_KV_RANK:]
    k_nope = jnp.einsum('btl,lhn->bthn', lat, w_uk).astype(f32)
    kpe_f = kpe_raw.astype(f32)
    ms = (jnp.sum(k_nope * k_nope, axis=-1) + jnp.sum(kpe_f * kpe_f, axis=-1)[..., None]) / MLA_QK
    kinv = lax.rsqrt(ms + RMS_EPS)
    kpe = rope(kpe_raw * g_k[MLA_NOPE:], pos)
    segs = [(lat, kpe, kinv, pos)]
    if past is not None:
        segs.append((past[0], past[1], past[2], None))
    ctx = sweep_query_blocks(lambda qs, pb: mla_attend(qs, pb, segs), (q_abs, q_pe), pos)
    o = jnp.einsum('bthl,lhv->bthv', ctx, w_uv).reshape(b, t, MLA_HEADS * MLA_V)
    out = o.astype(h.dtype) @ w_o
    return out.astype(h.dtype), lat, kpe, kinv


def sq_relu_mlp(h, w_up, w_down):
    a = jax.nn.relu(h @ w_up)
    return (a * a) @ w_down


def gather_pages(pool, j, page_table):
    g = pool[j, page_table]
    return g.reshape((g.shape[0], g.shape[1] * g.shape[2]) + g.shape[3:])


def setup_inputs(seed: int = 0) -> dict:
    key = jax.random.key(seed)
    keys = list(jax.random.split(key, 64))
    f32 = jnp.float32

    def nk():
        return keys.pop()

    def nrm(shape, scale=1.0):
        return jax.random.normal(nk(), shape, f32) * scale

    def gain(shape):
        return 1.0 + nrm(shape, 0.02)

    n_pages = PAST_LEN // PAGE_SIZE
    n_used = DEC_BATCH * n_pages
    n_pool = n_used + max(1, n_used // 4)
    page_table = jax.random.permutation(nk(), n_pool)[:n_used].reshape(DEC_BATCH, n_pages).astype(jnp.int32)
    a_im0 = jnp.pi * jnp.arange(S5_STATE, dtype=f32)
    return {
        'x_prompt': nrm((BATCH, SEQ, D_MODEL)),
        'x_sample': nrm((DEC_BATCH, DEC_SEQ, D_MODEL)),
        'cache_sb_k': nrm((N_SB_LAYERS, n_pool, PAGE_SIZE, SB_KV_HEADS, SB_HEAD_DIM)),
        'cache_sb_v': nrm((N_SB_LAYERS, n_pool, PAGE_SIZE, SB_KV_HEADS, SB_HEAD_DIM)),
        'cache_mla_latent': nrm((N_MLA_LAYERS, n_pool, PAGE_SIZE, MLA_KV_RANK)),
        'cache_mla_kpe': nrm((N_MLA_LAYERS, n_pool, PAGE_SIZE, MLA_ROPE)),
        'cache_mla_kinv': jax.random.uniform(nk(), (N_MLA_LAYERS, n_pool, PAGE_SIZE, MLA_HEADS), f32, 0.5, 1.5),
        'state_s5_re': nrm((N_S5_LAYERS, DEC_BATCH, S5_GROUPS, S5_STATE), 0.1),
        'state_s5_im': nrm((N_S5_LAYERS, DEC_BATCH, S5_GROUPS, S5_STATE), 0.1),
        'page_table': page_table,
        'ln_mix': gain((DEPTH, D_MODEL)),
        'ln_mlp': gain((DEPTH, D_MODEL)),
        'w_up': nrm((DEPTH, D_MODEL, D_FF), D_MODEL ** -0.5),
        'w_down': nrm((DEPTH, D_FF, D_MODEL), 0.5 * D_FF ** -0.5),
        'sb_w_qkv': nrm((N_SB_LAYERS, D_MODEL, (SB_HEADS + 2 * SB_KV_HEADS) * SB_HEAD_DIM), D_MODEL ** -0.5),
        'sb_q_gain': gain((N_SB_LAYERS, SB_HEAD_DIM)),
        'sb_k_gain': gain((N_SB_LAYERS, SB_HEAD_DIM)),
        'sb_w_o': nrm((N_SB_LAYERS, SB_HEADS * SB_HEAD_DIM, D_MODEL), (SB_HEADS * SB_HEAD_DIM) ** -0.5),
        's5_a_re': -0.5 + nrm((N_S5_LAYERS, S5_GROUPS, S5_STATE), 0.01),
        's5_a_im': a_im0 + nrm((N_S5_LAYERS, S5_GROUPS, S5_STATE), 0.01),
        's5_log_dt': jax.random.uniform(nk(), (N_S5_LAYERS, S5_GROUPS), f32, math.log(S5_DT_MIN), math.log(S5_DT_MAX)),
        's5_b_re': nrm((N_S5_LAYERS, S5_GROUPS, S5_STATE, S5_GROUP_CH), (2 * S5_GROUP_CH) ** -0.5),
        's5_b_im': nrm((N_S5_LAYERS, S5_GROUPS, S5_STATE, S5_GROUP_CH), (2 * S5_GROUP_CH) ** -0.5),
        's5_c_re': nrm((N_S5_LAYERS, S5_GROUPS, S5_GROUP_CH, S5_STATE), S5_STATE ** -0.5),
        's5_c_im': nrm((N_S5_LAYERS, S5_GROUPS, S5_GROUP_CH, S5_STATE), S5_STATE ** -0.5),
        's5_d': nrm((N_S5_LAYERS, D_MODEL)),
        's5_w_glu': nrm((N_S5_LAYERS, D_MODEL, 2 * D_MODEL), D_MODEL ** -0.5),
        'mla_w_dq': nrm((N_MLA_LAYERS, D_MODEL, MLA_Q_RANK), D_MODEL ** -0.5),
        'mla_q_ln': gain((N_MLA_LAYERS, MLA_Q_RANK)),
        'mla_w_uq': nrm((N_MLA_LAYERS, MLA_Q_RANK, MLA_HEADS * MLA_QK), MLA_Q_RANK ** -0.5),
        'mla_w_dkv': nrm((N_MLA_LAYERS, D_MODEL, MLA_KV_RANK + MLA_ROPE), D_MODEL ** -0.5),
        'mla_kv_ln': gain((N_MLA_LAYERS, MLA_KV_RANK)),
        'mla_w_uk': nrm((N_MLA_LAYERS, MLA_KV_RANK, MLA_HEADS, MLA_NOPE), MLA_KV_RANK ** -0.5),
        'mla_w_uv': nrm((N_MLA_LAYERS, MLA_KV_RANK, MLA_HEADS, MLA_V), MLA_KV_RANK ** -0.5),
        'mla_q_gain': gain((N_MLA_LAYERS, MLA_QK)),
        'mla_k_gain': gain((N_MLA_LAYERS, MLA_QK)),
        'mla_w_o': nrm((N_MLA_LAYERS, MLA_HEADS * MLA_V, D_MODEL), (MLA_HEADS * MLA_V) ** -0.5),
    }


def reference(x_prompt, x_sample, cache_sb_k, cache_sb_v, cache_mla_latent, cache_mla_kpe, cache_mla_kinv,
              state_s5_re, state_s5_im, page_table,
              ln_mix, ln_mlp, w_up, w_down,
              sb_w_qkv, sb_q_gain, sb_k_gain, sb_w_o,
              s5_a_re, s5_a_im, s5_log_dt, s5_b_re, s5_b_im, s5_c_re, s5_c_im, s5_d, s5_w_glu,
              mla_w_dq, mla_q_ln, mla_w_uq, mla_w_dkv, mla_kv_ln, mla_w_uk, mla_w_uv, mla_q_gain, mla_k_gain, mla_w_o):
    past_len = page_table.shape[1] * PAGE_SIZE
    pos_p = jnp.arange(x_prompt.shape[1], dtype=jnp.int32)
    pos_s = past_len + jnp.arange(x_sample.shape[1], dtype=jnp.int32)
    xp, xs = x_prompt, x_sample
    sbk_p, sbv_p, sbk_s, sbv_s = [], [], [], []
    lat_p, kpe_p, kinv_p, lat_s, kpe_s, kinv_s = [], [], [], [], [], []
    s5r_p, s5i_p, s5r_s, s5i_s = [], [], [], []
    for i in range(DEPTH):
        kind, j = i % N_MIXERS, i // N_MIXERS
        hp = rms_norm(xp, ln_mix[i])
        hs = rms_norm(xs, ln_mix[i])
        if kind == 0:
            w = (sb_w_qkv[j], sb_q_gain[j], sb_k_gain[j], sb_w_o[j])
            op, k_new, v_new = sb_mixer(hp, pos_p, *w)
            sbk_p.append(k_new)
            sbv_p.append(v_new)
            os_, k_new, v_new = sb_mixer(hs, pos_s, *w,
                                         k_past=gather_pages(cache_sb_k, j, page_table),
                                         v_past=gather_pages(cache_sb_v, j, page_table))
            sbk_s.append(k_new)
            sbv_s.append(v_new)
        elif kind == 1:
            w = (s5_a_re[j], s5_a_im[j], s5_log_dt[j], s5_b_re[j], s5_b_im[j], s5_c_re[j], s5_c_im[j], s5_d[j], s5_w_glu[j])
            zeros = jnp.zeros((xp.shape[0], S5_GROUPS, S5_STATE), jnp.float32)
            op, st_r, st_i = s5_mixer(hp, *w, zeros, zeros)
            s5r_p.append(st_r)
            s5i_p.append(st_i)
            os_, st_r, st_i = s5_mixer(hs, *w, state_s5_re[j], state_s5_im[j])
            s5r_s.append(st_r)
            s5i_s.append(st_i)
        else:
            w = (mla_w_dq[j], mla_q_ln[j], mla_w_uq[j], mla_w_dkv[j], mla_kv_ln[j], mla_w_uk[j], mla_w_uv[j],
                 mla_q_gain[j], mla_k_gain[j], mla_w_o[j])
            op, la, kp, ki = mla_mixer(hp, pos_p, *w)
            lat_p.append(la)
            kpe_p.append(kp)
            kinv_p.append(ki)
            past = (gather_pages(cache_mla_latent, j, page_table),
                    gather_pages(cache_mla_kpe, j, page_table),
                    gather_pages(cache_mla_kinv, j, page_table))
            os_, la, kp, ki = mla_mixer(hs, pos_s, *w, past=past)
            lat_s.append(la)
            kpe_s.append(kp)
            kinv_s.append(ki)
        xp = xp + op
        xs = xs + os_
        xp = xp + sq_relu_mlp(rms_norm(xp, ln_mlp[i]), w_up[i], w_down[i])
        xs = xs + sq_relu_mlp(rms_norm(xs, ln_mlp[i]), w_up[i], w_down[i])
    return (xp, xs,
            jnp.stack(sbk_p), jnp.stack(sbv_p), jnp.stack(sbk_s), jnp.stack(sbv_s),
            jnp.stack(lat_p), jnp.stack(kpe_p), jnp.stack(kinv_p),
            jnp.stack(lat_s), jnp.stack(kpe_s), jnp.stack(kinv_s),
            jnp.stack(s5r_p), jnp.stack(s5i_p), jnp.stack(s5r_s), jnp.stack(s5i_s))
```

```python
import functools
import math

import numpy as np
import jax
import jax.numpy as jnp
from jax import lax
from jax.experimental import pallas as pl
from jax.experimental.pallas import tpu as pltpu

F32 = jnp.float32
BF16 = jnp.bfloat16

D_MODEL = 1024
DEPTH = 4
PAGE_SIZE = 128
N_MIXERS = 3
D_FF = 4 * D_MODEL
SB_HEADS = 16
SB_KV_HEADS = 4
SB_HEAD_DIM = 64
SB_GROUP = SB_HEADS // SB_KV_HEADS
SB_SCALE = SB_HEAD_DIM ** -0.5
SB_NQ = SB_HEADS * SB_HEAD_DIM
SB_NKV = SB_KV_HEADS * SB_HEAD_DIM
S5_GROUP_CH = 16
S5_GROUPS = D_MODEL // S5_GROUP_CH
S5_STATE = 64
S5_DIM = S5_GROUPS * S5_STATE
MLA_HEADS = 16
MLA_Q_RANK = 384
MLA_KV_RANK = 256
MLA_NOPE = 64
MLA_ROPE = 32
MLA_V = 64
MLA_QK = MLA_NOPE + MLA_ROPE
MLA_SCALE = MLA_QK ** -0.5
ROPE_THETA = 10000.0
RMS_EPS = 1e-6
NEG_INF = -1e30

LANES = 128
MLA_SEG = LANES
VMEM_LIMIT_BYTES = 56 * 1024 * 1024


def _cparams(*sem):
    return pltpu.CompilerParams(dimension_semantics=sem, vmem_limit_bytes=VMEM_LIMIT_BYTES)


def _rms(x, g):
    xf = x.astype(F32)
    return xf * lax.rsqrt(jnp.mean(xf * xf, axis=-1, keepdims=True) + RMS_EPS) * g


def _dot(a, b):
    return jnp.dot(a, b, preferred_element_type=F32)


def _dot_nt(a, b):
    return lax.dot_general(a, b, (((1,), (1,)), ((), ())), preferred_element_type=F32)


def _split(x):
    hi = x.astype(BF16)
    lo = (x - hi.astype(F32)).astype(BF16)
    return hi, lo


def _sdot(x, s):
    hi, lo = _split(x)
    return _dot(hi, s) + _dot(lo, s)


def _sdot_nt(s, x):
    hi, lo = _split(x)
    return _dot_nt(s, hi) + _dot_nt(s, lo)


def _row_tile(m, want):
    t = min(m, want)
    assert m % t == 0
    return t


def _mlp_kernel(x_ref, g_ref, wu_ref, wd_ref, o_ref, h_ref, acc_ref):
    f = pl.program_id(1)

    @pl.when(f == 0)
    def _():
        h_ref[...] = _rms(x_ref[...], g_ref[...]).astype(BF16)
        acc_ref[...] = jnp.zeros_like(acc_ref)

    a = jnp.maximum(_dot(h_ref[...], wu_ref[...]), 0.0)
    acc_ref[...] += _dot((a * a).astype(BF16), wd_ref[...])

    @pl.when(f == pl.num_programs(1) - 1)
    def _():
        o_ref[...] = x_ref[...] + acc_ref[...]


def _mlp(x, g, wu, wd):
    m = x.shape[0]
    tm = _row_tile(m, 1024)
    tf = 512
    return pl.pallas_call(
        _mlp_kernel,
        grid=(m // tm, D_FF // tf),
        in_specs=[
            pl.BlockSpec((tm, D_MODEL), lambda i, f: (i, 0)),
            pl.BlockSpec((1, D_MODEL), lambda i, f: (0, 0)),
            pl.BlockSpec((D_MODEL, tf), lambda i, f: (0, f)),
            pl.BlockSpec((tf, D_MODEL), lambda i, f: (f, 0)),
        ],
        out_specs=pl.BlockSpec((tm, D_MODEL), lambda i, f: (i, 0)),
        out_shape=jax.ShapeDtypeStruct((m, D_MODEL), F32),
        scratch_shapes=[pltpu.VMEM((tm, D_MODEL), BF16), pltpu.VMEM((tm, D_MODEL), F32)],
        compiler_params=_cparams("parallel", "arbitrary"),
        name="mlp",
    )(x, g, wu, wd)


def _proj_res_kernel(x_ref, o_ref, w_ref, y_ref):
    y_ref[...] = x_ref[...] + _dot(o_ref[...], w_ref[...])


def _proj_res(x, o, w):
    m, k = o.shape
    tm = _row_tile(m, 1024)
    return pl.pallas_call(
        _proj_res_kernel,
        grid=(m // tm,),
        in_specs=[
            pl.BlockSpec((tm, D_MODEL), lambda i: (i, 0)),
            pl.BlockSpec((tm, k), lambda i: (i, 0)),
            pl.BlockSpec((k, D_MODEL), lambda i: (0, 0)),
        ],
        out_specs=pl.BlockSpec((tm, D_MODEL), lambda i: (i, 0)),
        out_shape=jax.ShapeDtypeStruct((m, D_MODEL), F32),
        compiler_params=_cparams("parallel"),
        name="proj_res",
    )(x, o, w)


def _seg_matrix(width, seg, cols=LANES):
    r = np.arange(width)[:, None] // seg
    c = np.arange(cols)[None, :]
    return jnp.asarray((r == c).astype(np.float32), dtype=BF16)


def _sb_proj_kernel(x_ref, g_ref, w_ref, seg_ref, segt_ref, gain_ref, q_ref, k_ref, v_ref, kt_ref, vt_ref):
    h = _rms(x_ref[...], g_ref[...]).astype(BF16)
    qkv = _dot(h, w_ref[...])
    nqk = SB_NQ + SB_NKV
    qk = qkv[:, :nqk]
    ms = _sdot(qk * qk, seg_ref[...]) * (1.0 / SB_HEAD_DIM)
    scale = _sdot(lax.rsqrt(ms + RMS_EPS), segt_ref[...])
    qkn = qk * scale * gain_ref[...]
    kn = qkn[:, SB_NQ:]
    v = qkv[:, nqk:]
    q_ref[...] = qkn[:, :SB_NQ].astype(BF16)
    k_ref[...] = kn
    v_ref[...] = v
    kt_ref[0] = kn.T
    vt_ref[0] = v.T


def _sb_proj(x, g, w, gq, gk, nseq):
    m = x.shape[0]
    t = m // nseq
    tm = _row_tile(t, 512)
    nt = t // tm
    nqk = SB_NQ + SB_NKV
    seg = _seg_matrix(nqk, SB_HEAD_DIM)
    gain = jnp.concatenate([jnp.tile(gq * SB_SCALE, SB_HEADS), jnp.tile(gk, SB_KV_HEADS)])[None, :]
    return pl.pallas_call(
        _sb_proj_kernel,
        grid=(m // tm,),
        in_specs=[
            pl.BlockSpec((tm, D_MODEL), lambda i: (i, 0)),
            pl.BlockSpec((1, D_MODEL), lambda i: (0, 0)),
            pl.BlockSpec(w.shape, lambda i: (0, 0)),
            pl.BlockSpec(seg.shape, lambda i: (0, 0)),
            pl.BlockSpec(seg.T.shape, lambda i: (0, 0)),
            pl.BlockSpec(gain.shape, lambda i: (0, 0)),
        ],
        out_specs=[
            pl.BlockSpec((tm, SB_NQ), lambda i: (i, 0)),
            pl.BlockSpec((tm, SB_NKV), lambda i: (i, 0)),
            pl.BlockSpec((tm, SB_NKV), lambda i: (i, 0)),
            pl.BlockSpec((1, SB_NKV, tm), lambda i: (i // nt, 0, i % nt)),
            pl.BlockSpec((1, SB_NKV, tm), lambda i: (i // nt, 0, i % nt)),
        ],
        out_shape=[
            jax.ShapeDtypeStruct((m, SB_NQ), BF16),
            jax.ShapeDtypeStruct((m, SB_NKV), F32),
            jax.ShapeDtypeStruct((m, SB_NKV), F32),
            jax.ShapeDtypeStruct((nseq, SB_NKV, t), F32),
            jax.ShapeDtypeStruct((nseq, SB_NKV, t), F32),
        ],
        compiler_params=_cparams("parallel"),
        name="sb_proj",
    )(x, g, w, seg, seg.T, gain)


def _tri_ones():
    j = np.arange(LANES)[:, None]
    s = np.arange(LANES)[None, :]
    u = np.concatenate([(j > s).astype(np.float32), np.ones((LANES, LANES), np.float32)], axis=1)
    return jnp.asarray(np.concatenate([u, u], axis=0), dtype=BF16)


def _log_sigmoid_neg(z):
    return jnp.minimum(-z, 0.0) - jnp.log(1.0 + jnp.exp(-jnp.abs(z)))


def _tile_tail(lneg, uo, c):
    hi, lo = _split(lneg)
    nsub = lneg.shape[1] // LANES
    tails = [None] * nsub
    for sb in reversed(range(nsub)):
        sl = slice(sb * LANES, (sb + 1) * LANES)
        res = _dot(jnp.concatenate([hi[:, sl], lo[:, sl]], axis=1), uo)
        tails[sb] = res[:, :LANES] + c
        c = c + res[:, LANES:]
    return jnp.concatenate(tails, axis=1), c


def _sb_attn_kernel(qi_ref, kv_ref, q_ref, kt_ref, vt_ref, uo_ref, o_ref, acc_ref, c_ref, *, tq, tk):
    s = pl.program_id(2)
    i = qi_ref[s]
    kv = kv_ref[s]
    r = tq // tk
    first = kv == (i + 1) * r - 1
    last = kv == 0

    @pl.when(first)
    def _():
        acc_ref[...] = jnp.zeros_like(acc_ref)
        c_ref[...] = jnp.zeros_like(c_ref)

    def step(masked):
        kt = kt_ref[0].astype(BF16)
        vt = vt_ref[0].astype(BF16)
        uo = uo_ref[...]
        if masked:
            row = lax.broadcasted_iota(jnp.int32, (tq, tk), 0) + i * tq
            col = lax.broadcasted_iota(jnp.int32, (tq, tk), 1) + kv * tk
            valid = col < row
        for g in range(SB_GROUP):
            qg = q_ref[0, :, g * SB_HEAD_DIM:(g + 1) * SB_HEAD_DIM]
            z = _dot(qg, kt)
            lneg = _log_sigmoid_neg(z)
            if masked:
                lneg = jnp.where(valid, lneg, 0.0)
            tail, c = _tile_tail(lneg, uo, c_ref[g])
            c_ref[g] = c
            w = jnp.exp(z + lneg + tail)
            if masked:
                w = jnp.where(valid, w, 0.0)
            acc_ref[g] += _dot_nt(w.astype(BF16), vt)

    diag = kv >= i * r

    @pl.when(diag)
    def _():
        step(True)

    @pl.when(jnp.logical_not(diag))
    def _():
        step(False)

    @pl.when(last)
    def _():
        o_ref[0] = jnp.concatenate([acc_ref[g] for g in range(SB_GROUP)], axis=1).astype(BF16)


def _causal_pairs(nq, r, reverse):
    qi, kv = [], []
    for i in range(nq):
        ks = list(range((i + 1) * r))
        if reverse:
            ks = ks[::-1]
        qi += [i] * len(ks)
        kv += ks
    return jnp.asarray(qi, jnp.int32), jnp.asarray(kv, jnp.int32)


def _sb_attn(q, kt, vt):
    b, t, _ = q.shape
    tq = _row_tile(t, 256)
    tk = _row_tile(tq, 256)
    qi, kv = _causal_pairs(t // tq, tq // tk, reverse=True)
    gw = SB_GROUP * SB_HEAD_DIM
    uo = _tri_ones()
    grid_spec = pltpu.PrefetchScalarGridSpec(
        num_scalar_prefetch=2,
        grid=(b, SB_KV_HEADS, int(qi.shape[0])),
        in_specs=[
            pl.BlockSpec((1, tq, gw), lambda bb, h, s, qi_r, kv_r: (bb, qi_r[s], h)),
            pl.BlockSpec((1, SB_HEAD_DIM, tk), lambda bb, h, s, qi_r, kv_r: (bb, h, kv_r[s])),
            pl.BlockSpec((1, SB_HEAD_DIM, tk), lambda bb, h, s, qi_r, kv_r: (bb, h, kv_r[s])),
            pl.BlockSpec(uo.shape, lambda bb, h, s, qi_r, kv_r: (0, 0)),
        ],
        out_specs=pl.BlockSpec((1, tq, gw), lambda bb, h, s, qi_r, kv_r: (bb, qi_r[s], h)),
        scratch_shapes=[
            pltpu.VMEM((SB_GROUP, tq, SB_HEAD_DIM), F32),
            pltpu.VMEM((SB_GROUP, tq, LANES), F32),
        ],
    )
    return pl.pallas_call(
        functools.partial(_sb_attn_kernel, tq=tq, tk=tk),
        grid_spec=grid_spec,
        out_shape=jax.ShapeDtypeStruct((b, t, SB_NQ), BF16),
        compiler_params=_cparams("parallel", "parallel", "arbitrary"),
        name="sb_attn",
    )(qi, kv, q, kt, vt, uo)


def _page_copies(pt_ref, seq, slot, *, npages, layer, pools, bufs, sem):
    out = []
    for n, (pool, buf) in enumerate(zip(pools, bufs)):
        for p in range(npages):
            pg = pt_ref[seq * npages + p]
            out.append(pltpu.make_async_copy(
                pool.at[layer, pg], buf.at[slot, :, pl.ds(p * PAGE_SIZE, PAGE_SIZE)], sem.at[n, slot]))
    return out


def _sb_decode_kernel(pt_ref, vs_ref, qbd_ref, qbdt_ref, knr_ref, knc_ref, vnc_ref, uo_ref, kc_ref, vc_ref,
                      o_ref, kbuf, vbuf, sem, *, npages, layer):
    b = pl.program_id(0)
    nb = pl.num_programs(0)
    slot = b % 2
    copies = functools.partial(_page_copies, pt_ref, npages=npages, layer=layer, pools=(kc_ref, vc_ref),
                               bufs=(kbuf, vbuf), sem=sem)

    @pl.when(b == 0)
    def _():
        for cp in copies(0, 0):
            cp.start()

    @pl.when(b + 1 < nb)
    def _():
        for cp in copies(b + 1, 1 - slot):
            cp.start()

    for cp in copies(b, slot):
        cp.wait()

    qbd = qbd_ref[0]
    nh = qbd.shape[0]
    self_valid = vs_ref[0] < vs_ref[1]
    z_col = jnp.sum(qbd.astype(F32) * knr_ref[0], axis=1, keepdims=True)
    l_col = jnp.where(self_valid, _log_sigmoid_neg(z_col), 0.0)
    z_row = jnp.sum(qbdt_ref[0].astype(F32) * knc_ref[0], axis=0, keepdims=True)
    w_row = jnp.where(self_valid, jnp.exp(z_row + _log_sigmoid_neg(z_row)), 0.0)

    z_all = _dot(qbd, kbuf[slot])
    z = jnp.concatenate([z_all[:, p * PAGE_SIZE:(p + 1) * PAGE_SIZE] for p in range(npages)], axis=0)
    lneg = _log_sigmoid_neg(z)
    hi, lo = _split(lneg)
    res = _dot(jnp.concatenate([hi, lo], axis=1), uo_ref[...])
    c = jnp.broadcast_to(l_col, (nh, PAGE_SIZE))
    tails = [None] * npages
    for p in reversed(range(npages)):
        rs = slice(p * nh, (p + 1) * nh)
        tails[p] = res[rs, :PAGE_SIZE] + c
        c = c + res[rs, PAGE_SIZE:]
    w = jnp.exp(z + lneg + jnp.concatenate(tails, axis=0))
    w_all = jnp.concatenate([w[p * nh:(p + 1) * nh] for p in range(npages)], axis=1)
    o_ref[0] = _dot_nt(vbuf[slot], w_all) + vnc_ref[0] * w_row


def _sb_decode(qbd, k_new, v_new, kc, vc, layer, page_table, self_pos):
    nb, nh, kw = qbd.shape
    npages = page_table.shape[1]
    s = npages * PAGE_SIZE
    uo = _tri_ones()
    im = lambda b, pt, vs: (b, 0, 0)
    grid_spec = pltpu.PrefetchScalarGridSpec(
        num_scalar_prefetch=2,
        grid=(nb,),
        in_specs=[
            pl.BlockSpec((1, nh, kw), im),
            pl.BlockSpec((1, kw, nh), im),
            pl.BlockSpec((1, 1, kw), im),
            pl.BlockSpec((1, kw, 1), im),
            pl.BlockSpec((1, kw, 1), im),
            pl.BlockSpec(uo.shape, lambda b, pt, vs: (0, 0)),
            pl.BlockSpec(memory_space=pl.ANY),
            pl.BlockSpec(memory_space=pl.ANY),
        ],
        out_specs=pl.BlockSpec((1, kw, nh), im),
        scratch_shapes=[
            pltpu.VMEM((2, kw, s), F32),
            pltpu.VMEM((2, kw, s), F32),
            pltpu.SemaphoreType.DMA((2, 2)),
        ],
    )
    return pl.pallas_call(
        functools.partial(_sb_decode_kernel, npages=npages, layer=layer),
        grid_spec=grid_spec,
        out_shape=jax.ShapeDtypeStruct((nb, kw, nh), F32),
        compiler_params=_cparams("arbitrary"),
        name="sb_decode",
    )(page_table.reshape(-1), self_pos, qbd, jnp.transpose(qbd, (0, 2, 1)), k_new[:, None, :],
      k_new[:, :, None], v_new[:, :, None], uo, kc, vc)


def _s5_disc_kernel(are_ref, aim_ref, ldt_ref, arex_ref, aimx_ref, ldtx_ref, bre_ref, bim_ref,
                    abr_ref, abi_ref, bbr_ref, bbi_ref):
    def zoh(a_re, a_im, log_dt):
        dt = jnp.exp(log_dt)
        mag = jnp.exp(dt * a_re)
        ab_re = mag * jnp.cos(dt * a_im)
        ab_im = mag * jnp.sin(dt * a_im)
        den = a_re * a_re + a_im * a_im
        xr, yi = ab_re - 1.0, ab_im
        return ab_re, ab_im, (xr * a_re + yi * a_im) / den, (yi * a_re - xr * a_im) / den

    ab_re, ab_im, _, _ = zoh(are_ref[...], aim_ref[...], ldt_ref[...])
    abr_ref[...] = ab_re
    abi_ref[...] = ab_im
    _, _, c_re, c_im = zoh(arex_ref[...], aimx_ref[...], ldtx_ref[...])
    b_re = bre_ref[...]
    b_im = bim_ref[...]
    bbr_ref[...] = c_re * b_re - c_im * b_im
    bbi_ref[...] = c_re * b_im + c_im * b_re


def _s5_disc(a_re, a_im, log_dt, b_re, b_im):
    g, p = a_re.shape
    c = b_re.shape[-1]
    rep = lambda a: jnp.repeat(a, c, axis=1)
    ldt = jnp.broadcast_to(log_dt[:, None], (g, p))
    args = (a_re, a_im, ldt, rep(a_re), rep(a_im), rep(ldt), b_re.reshape(g, p * c), b_im.reshape(g, p * c))
    small = jax.ShapeDtypeStruct((g, p), F32)
    wide = jax.ShapeDtypeStruct((g, p * c), F32)
    return pl.pallas_call(
        _s5_disc_kernel,
        out_shape=[small, small, wide, wide],
        name="s5_disc",
    )(*args)


def _s5_block_weights(bb_re, bb_im, c_re, c_im):
    g, c, p = S5_GROUPS, S5_GROUP_CH, S5_STATE
    nblk = 4
    gl = g // nblk
    eye = jnp.eye(gl, dtype=F32)

    def in_blk(bb):
        bb = bb.reshape(nblk, gl, p, c)
        return jnp.einsum('kgpc,gh->kgchp', bb, eye).reshape(nblk, gl * c, gl * p).astype(BF16)

    def out_blk(cc):
        cc = cc.reshape(nblk, gl, c, p)
        return jnp.einsum('kgcp,gh->kgphc', cc, eye).reshape(nblk, gl * p, gl * c).astype(BF16)

    return in_blk(bb_re), in_blk(bb_im), out_blk(c_re), out_blk(c_im)


def _s5_kernel(x_ref, g_ref, bre_ref, bim_ref, abr_ref, abi_ref, h0r_ref, h0i_ref, cre_ref, cim_ref,
               d_ref, wg_ref, o_ref, sr_ref, si_ref, str_ref, sti_ref, hr_ref, hi_ref, *, sequential):
    nblk = bre_ref.shape[0]
    cw = D_MODEL // nblk
    sw = S5_DIM // nblk
    x = x_ref[0]
    tt = x.shape[0]
    u = _rms(x, g_ref[...])
    ub = u.astype(BF16)
    for k in range(nblk):
        uk = ub[:, k * cw:(k + 1) * cw]
        str_ref[:, k * sw:(k + 1) * sw] = _dot(uk, bre_ref[k])
        sti_ref[:, k * sw:(k + 1) * sw] = _dot(uk, bim_ref[k])

    if sequential:
        @pl.when(pl.program_id(1) == 0)
        def _():
            hr_ref[...] = h0r_ref[0]
            hi_ref[...] = h0i_ref[0]

        for k in range(nblk):
            cs = slice(k * sw, (k + 1) * sw)
            ar = abr_ref[:, cs]
            ai = abi_ref[:, cs]

            def body(r, carry):
                hr, hi = carry
                br = str_ref[pl.ds(r, 1), cs]
                bi = sti_ref[pl.ds(r, 1), cs]
                nr = ar * hr - ai * hi + br
                ni = ar * hi + ai * hr + bi
                str_ref[pl.ds(r, 1), cs] = nr
                sti_ref[pl.ds(r, 1), cs] = ni
                return nr, ni

            hr, hi = lax.fori_loop(0, tt, body, (hr_ref[:, cs], hi_ref[:, cs]))
            hr_ref[:, cs] = hr
            hi_ref[:, cs] = hi
        sr_ref[0] = hr_ref[...]
        si_ref[0] = hi_ref[...]
    else:
        ar = abr_ref[...]
        ai = abi_ref[...]
        h0r = h0r_ref[0]
        h0i = h0i_ref[0]
        nr = ar * h0r - ai * h0i + str_ref[...]
        ni = ar * h0i + ai * h0r + sti_ref[...]
        str_ref[...] = nr
        sti_ref[...] = ni
        sr_ref[0] = nr
        si_ref[0] = ni

    ys = []
    for k in range(nblk):
        cs = slice(k * sw, (k + 1) * sw)
        ys.append(_dot(str_ref[:, cs].astype(BF16), cre_ref[k]) - _dot(sti_ref[:, cs].astype(BF16), cim_ref[k]))
    y = jnp.concatenate(ys, axis=1) + d_ref[...] * u
    y = jax.nn.gelu(y)
    vg = _dot(y.astype(BF16), wg_ref[...])
    o_ref[0] = x + vg[:, :D_MODEL] * jax.nn.sigmoid(vg[:, D_MODEL:])


def _s5(x, g, wts, abr, abi, h0r, h0i, d, wglu, sequential):
    b, t, _ = x.shape
    tt = _row_tile(t, 256)
    bre, bim, cre, cim = wts
    srow = 1 if sequential else tt
    st_spec = pl.BlockSpec((1, srow, S5_DIM), (lambda bb, i: (bb, 0, 0)) if sequential else (lambda bb, i: (bb, i, 0)))
    full = lambda a: pl.BlockSpec(a.shape, lambda bb, i: (0,) * a.ndim)
    return pl.pallas_call(
        functools.partial(_s5_kernel, sequential=sequential),
        grid=(b, t // tt),
        in_specs=[
            pl.BlockSpec((1, tt, D_MODEL), lambda bb, i: (bb, i, 0)),
            full(g), full(bre), full(bim), full(abr), full(abi),
            st_spec, st_spec,
            full(cre), full(cim), full(d), full(wglu),
        ],
        out_specs=[pl.BlockSpec((1, tt, D_MODEL), lambda bb, i: (bb, i, 0)), st_spec, st_spec],
        out_shape=[
            jax.ShapeDtypeStruct((b, t, D_MODEL), F32),
            jax.ShapeDtypeStruct((b, t if not sequential else 1, S5_DIM), F32),
            jax.ShapeDtypeStruct((b, t if not sequential else 1, S5_DIM), F32),
        ],
        scratch_shapes=[
            pltpu.VMEM((tt, S5_DIM), F32), pltpu.VMEM((tt, S5_DIM), F32),
            pltpu.VMEM((1, S5_DIM), F32), pltpu.VMEM((1, S5_DIM), F32),
        ],
        compiler_params=_cparams("parallel", "arbitrary"),
        name="s5_seq" if sequential else "s5_step",
    )(x, g, bre, bim, abr, abi, h0r, h0i, cre, cim, d, wglu)


def _seg_cols(w_nope, w_pe):
    k, h, _ = w_nope.shape
    w_pe = jnp.broadcast_to(w_pe, (k, h, MLA_ROPE))
    pad = jnp.zeros((k, h, MLA_SEG - MLA_QK), w_nope.dtype)
    return jnp.concatenate([w_nope, w_pe, pad], axis=-1).reshape(k, h * MLA_SEG)


def _seg_vec(v_nope, v_pe, fill=0.0):
    pad = jnp.full((MLA_SEG - MLA_QK,), fill, F32)
    return jnp.concatenate([v_nope, v_pe, pad])[None, :]


def _rope_tables(pos):
    half = MLA_ROPE // 2
    freqs = ROPE_THETA ** (-jnp.arange(half, dtype=F32) / half)
    ang = pos.astype(F32)[:, None] * freqs[None, :]
    cos, sin = jnp.cos(ang), jnp.sin(ang)
    n = pos.shape[0]
    one = jnp.ones((n, MLA_NOPE), F32)
    zero = jnp.zeros((n, MLA_NOPE), F32)
    z32 = jnp.zeros((n, MLA_SEG - MLA_QK), F32)
    z16 = jnp.zeros((n, half), F32)
    c = jnp.concatenate([one, cos, cos, z32], axis=1)
    s_lo = jnp.concatenate([zero, -sin, z16, z32], axis=1)
    s_hi = jnp.concatenate([zero, z16, sin, z32], axis=1)
    return c, s_lo, s_hi


def _rope_seg(x, c, s_lo, s_hi):
    half = MLA_ROPE // 2
    up = pltpu.roll(x, MLA_SEG - half, 1)
    dn = pltpu.roll(x, half, 1)
    return x * c + up * s_lo + dn * s_hi


def _mla_proj_kernel(x_ref, g_ref, wdq_ref, gql_ref, wuq_ref, gq_ref, gkq_ref, wdkv_ref, gkvl_ref,
                     wuk_ref, wuv_ref, gkpe_ref, segt_ref, c_ref, slo_ref, shi_ref,
                     q_ref, lat_ref, kpe_ref, kinvt_ref, kt_ref, v_ref):
    h = _rms(x_ref[...], g_ref[...]).astype(BF16)
    c, s_lo, s_hi = c_ref[...], slo_ref[...], shi_ref[...]
    cq = _rms(_dot(h, wdq_ref[...]), gql_ref[...]).astype(BF16)
    qraw = _dot(cq, wuq_ref[...])
    for hh in range(MLA_HEADS):
        qs = qraw[:, hh * MLA_SEG:(hh + 1) * MLA_SEG]
        qs = qs * lax.rsqrt(jnp.sum(qs * qs, axis=1, keepdims=True) * (1.0 / MLA_QK) + RMS_EPS) * gq_ref[...]
        qs = _rope_seg(qs * gkq_ref[...], c, s_lo, s_hi)
        q_ref[:, hh * MLA_SEG:(hh + 1) * MLA_SEG] = qs.astype(BF16)
    ckv = _dot(h, wdkv_ref[...])
    lat = _rms(ckv[:, :MLA_KV_RANK], gkvl_ref[...])
    lat_ref[...] = lat
    latb = lat.astype(BF16)
    kpe_raw = ckv[:, MLA_KV_RANK:]
    kn = _dot(latb, wuk_ref[...])
    ones = jnp.ones((8, MLA_SEG), BF16)
    pe_ss = _sdot_nt(ones, kpe_raw * kpe_raw)[:1]
    ms = (_sdot_nt(segt_ref[...], kn * kn) + pe_ss) * (1.0 / MLA_QK)
    kinvt = lax.rsqrt(ms + RMS_EPS)
    kinvt_ref[0] = kinvt
    kpe = _rope_seg(kpe_raw * gkpe_ref[...], c, s_lo, s_hi)
    kpe_ref[0] = kpe.T
    for hh in range(MLA_HEADS):
        ks = (kn[:, hh * MLA_SEG:(hh + 1) * MLA_SEG] + kpe).T
        kt_ref[0, hh] = (ks * (kinvt[hh:hh + 1] * MLA_SCALE)).astype(BF16)
    v_ref[...] = _dot(latb, wuv_ref[...]).astype(BF16)


def _mla_weights(w_dq, g_ql, w_uq, w_dkv, g_kvl, w_uk, w_uv, g_q, g_k):
    half = MLA_ROPE // 2
    uq = w_uq.reshape(MLA_Q_RANK, MLA_HEADS, MLA_QK)
    wuq = _seg_cols(uq[..., :MLA_NOPE], uq[..., MLA_NOPE:]).astype(BF16)
    kvn = jnp.zeros((D_MODEL, 1, MLA_NOPE), F32)
    wdkv = jnp.concatenate([w_dkv[:, :MLA_KV_RANK], _seg_cols(kvn, w_dkv[:, None, MLA_KV_RANK:])], axis=1).astype(BF16)
    wuk = _seg_cols(w_uk, jnp.zeros((MLA_KV_RANK, 1, MLA_ROPE), F32)).astype(BF16)
    return dict(
        wdq=w_dq.astype(BF16), gql=g_ql[None, :], wuq=wuq,
        gq=_seg_vec(g_q[:MLA_NOPE], g_q[MLA_NOPE:]),
        gkq=_seg_vec(g_k[:MLA_NOPE], jnp.ones((MLA_ROPE,), F32)),
        wdkv=wdkv, gkvl=g_kvl[None, :], wuk=wuk,
        wuv=w_uv.reshape(MLA_KV_RANK, MLA_HEADS * MLA_V).astype(BF16),
        gkpe=_seg_vec(jnp.zeros((MLA_NOPE,), F32), g_k[MLA_NOPE:]),
        segt=_seg_matrix(MLA_HEADS * MLA_SEG, MLA_SEG, cols=MLA_HEADS).T,
        wukt=jnp.transpose(w_uk, (1, 2, 0)).astype(BF16),
        wuvh=jnp.transpose(w_uv, (1, 0, 2)).astype(BF16),
    )


def _mla_proj(x, g, mw, pos, nseq):
    m = x.shape[0]
    t = m // nseq
    tm = _row_tile(t, 256)
    nt = t // tm
    tabs = _rope_tables(pos)
    hs = MLA_HEADS * MLA_SEG
    full = lambda a: pl.BlockSpec(a.shape, lambda i: (0,) * a.ndim)
    tab_spec = pl.BlockSpec((tm, MLA_SEG), lambda i: (i % nt, 0))
    tmaj = lambda rows: pl.BlockSpec((1, rows, tm), lambda i: (i // nt, 0, i % nt))
    ws = [mw[k] for k in ("wdq", "gql", "wuq", "gq", "gkq", "wdkv", "gkvl", "wuk", "wuv", "gkpe", "segt")]
    return pl.pallas_call(
        _mla_proj_kernel,
        grid=(m // tm,),
        in_specs=[pl.BlockSpec((tm, D_MODEL), lambda i: (i, 0)), full(g)] + [full(w) for w in ws]
        + [tab_spec] * 3,
        out_specs=[
            pl.BlockSpec((tm, hs), lambda i: (i, 0)),
            pl.BlockSpec((tm, MLA_KV_RANK), lambda i: (i, 0)),
            tmaj(MLA_SEG),
            tmaj(MLA_HEADS),
            pl.BlockSpec((1, MLA_HEADS, MLA_SEG, tm), lambda i: (i // nt, 0, 0, i % nt)),
            pl.BlockSpec((tm, MLA_HEADS * MLA_V), lambda i: (i, 0)),
        ],
        out_shape=[
            jax.ShapeDtypeStruct((m, hs), BF16),
            jax.ShapeDtypeStruct((m, MLA_KV_RANK), F32),
            jax.ShapeDtypeStruct((nseq, MLA_SEG, t), F32),
            jax.ShapeDtypeStruct((nseq, MLA_HEADS, t), F32),
            jax.ShapeDtypeStruct((nseq, MLA_HEADS, MLA_SEG, t), BF16),
            jax.ShapeDtypeStruct((m, MLA_HEADS * MLA_V), BF16),
        ],
        compiler_params=_cparams("parallel"),
        name="mla_proj",
    )(x, g, *ws, *tabs)


def _mla_attn_kernel(qi_ref, kv_ref, q_ref, kt_ref, v_ref, o_ref, m_ref, l_ref, acc_ref, *, tq, tk):
    s = pl.program_id(2)
    i = qi_ref[s]
    kv = kv_ref[s]
    r = tq // tk

    @pl.when(kv == 0)
    def _():
        m_ref[...] = jnp.full_like(m_ref, NEG_INF)
        l_ref[...] = jnp.zeros_like(l_ref)
        acc_ref[...] = jnp.zeros_like(acc_ref)

    lane = lax.broadcasted_iota(jnp.int32, (tq, LANES), 1)
    first_head = lane < MLA_V

    def step(masked):
        if masked:
            row = lax.broadcasted_iota(jnp.int32, (tq, tk), 0) + i * tq
            col = lax.broadcasted_iota(jnp.int32, (tq, tk), 1) + kv * tk
            valid = col <= row
        v = v_ref[0]
        pvs, alphas = [], []
        for hh in range(2):
            sc = _dot(q_ref[0, :, hh * MLA_SEG:(hh + 1) * MLA_SEG], kt_ref[0, hh])
            if masked:
                sc = jnp.where(valid, sc, NEG_INF)
            m_prev = m_ref[hh]
            m_new = jnp.maximum(m_prev, jnp.max(sc, axis=1, keepdims=True))
            alpha = jnp.exp(m_prev - m_new)
            p = jnp.exp(sc - m_new[:, :1])
            l_ref[hh] = alpha * l_ref[hh] + jnp.sum(p, axis=1, keepdims=True)
            m_ref[hh] = m_new
            pvs.append(_dot(p.astype(BF16), v))
            alphas.append(alpha)
        acc_ref[...] = acc_ref[...] * jnp.where(first_head, alphas[0], alphas[1]) + jnp.where(first_head, pvs[0], pvs[1])

    diag = kv >= i * r

    @pl.when(diag)
    def _():
        step(True)

    @pl.when(jnp.logical_not(diag))
    def _():
        step(False)

    @pl.when(kv == (i + 1) * r - 1)
    def _():
        o_ref[0] = (acc_ref[...] / jnp.where(first_head, l_ref[0], l_ref[1])).astype(BF16)


def _mla_attn(q, kt, v):
    b, t, _ = q.shape
    tq = _row_tile(t, 256)
    tk = _row_tile(tq, 256)
    qi, kvt = _causal_pairs(t // tq, tq // tk, reverse=False)
    grid_spec = pltpu.PrefetchScalarGridSpec(
        num_scalar_prefetch=2,
        grid=(b, MLA_HEADS // 2, int(qi.shape[0])),
        in_specs=[
            pl.BlockSpec((1, tq, 2 * MLA_SEG), lambda bb, h, s, qi_r, kv_r: (bb, qi_r[s], h)),
            pl.BlockSpec((1, 2, MLA_SEG, tk), lambda bb, h, s, qi_r, kv_r: (bb, h, 0, kv_r[s])),
            pl.BlockSpec((1, tk, 2 * MLA_V), lambda bb, h, s, qi_r, kv_r: (bb, kv_r[s], h)),
        ],
        out_specs=pl.BlockSpec((1, tq, 2 * MLA_V), lambda bb, h, s, qi_r, kv_r: (bb, qi_r[s], h)),
        scratch_shapes=[
            pltpu.VMEM((2, tq, LANES), F32),
            pltpu.VMEM((2, tq, LANES), F32),
            pltpu.VMEM((tq, LANES), F32),
        ],
    )
    return pl.pallas_call(
        functools.partial(_mla_attn_kernel, tq=tq, tk=tk),
        grid_spec=grid_spec,
        out_shape=jax.ShapeDtypeStruct((b, t, MLA_HEADS * MLA_V), BF16),
        compiler_params=_cparams("parallel", "parallel", "arbitrary"),
        name="mla_attn",
    )(qi, kvt, q, kt, v)


def _mla_qabs_kernel(q_ref, wukt_ref, qa_ref):
    for hh in range(MLA_HEADS):
        qn = q_ref[:, hh * MLA_SEG:hh * MLA_SEG + MLA_NOPE]
        qa_ref[hh] = _dot(qn, wukt_ref[hh])


def _mla_qabs(q, wukt):
    m = q.shape[0]
    return pl.pallas_call(
        _mla_qabs_kernel,
        out_shape=jax.ShapeDtypeStruct((MLA_HEADS, m, MLA_KV_RANK), F32),
        name="mla_qabs",
    )(q, wukt)


def _mla_decode_kernel(pt_ref, qa_ref, qp_ref, latn_ref, kpen_ref, kinvn_ref, lc_ref, pc_ref, ic_ref,
                       o_ref, lbuf, pbuf, ibuf, sem, *, npages, layer):
    b = pl.program_id(0)
    nb = pl.num_programs(0)
    slot = b % 2

    def copies(seq, sl):
        out = []
        for p in range(npages):
            pg = pt_ref[seq * npages + p]
            rows = pl.ds(p * PAGE_SIZE, PAGE_SIZE)
            out.append(pltpu.make_async_copy(lc_ref.at[layer, pg], lbuf.at[sl, rows, :], sem.at[0, sl]))
            out.append(pltpu.make_async_copy(pc_ref.at[layer, pg], pbuf.at[sl, :, rows], sem.at[1, sl]))
            out.append(pltpu.make_async_copy(ic_ref.at[layer, pg], ibuf.at[sl, :, rows], sem.at[2, sl]))
        return out

    @pl.when(b == 0)
    def _():
        for cp in copies(0, 0):
            cp.start()

    @pl.when(b + 1 < nb)
    def _():
        for cp in copies(b + 1, 1 - slot):
            cp.start()

    for cp in copies(b, slot):
        cp.wait()

    qa = qa_ref[0].astype(BF16)
    qp = qp_ref[0]
    lat = lbuf[slot]
    sc = (_dot_nt(qa, lat) + _dot(qp, pbuf[slot])) * MLA_SCALE * ibuf[slot]
    latn = latn_ref[0]
    rnd = lambda a: a.astype(BF16).astype(F32)
    s_self = (jnp.sum(rnd(qa) * rnd(latn), axis=1, keepdims=True)
              + jnp.sum(rnd(qp) * rnd(kpen_ref[0]), axis=1, keepdims=True))
    s_self = s_self * MLA_SCALE * kinvn_ref[0]
    m = jnp.maximum(jnp.max(sc, axis=1, keepdims=True), s_self)
    e = jnp.exp(sc - m)
    e_self = jnp.exp(s_self - m)
    den = jnp.sum(e, axis=1, keepdims=True) + e_self
    ctx = _dot(e, lat) + e_self * latn
    o_ref[0] = ctx / den


def _mla_decode(qa, qp, lat_new, kpe_new, kinv_new, lat_pool, kpe_pool, kinv_pool, layer, page_table):
    nb = qa.shape[0]
    npages = page_table.shape[1]
    s = npages * PAGE_SIZE
    per_seq = lambda a: pl.BlockSpec((1,) + a.shape[1:], lambda b, pt: (b, 0, 0))
    grid_spec = pltpu.PrefetchScalarGridSpec(
        num_scalar_prefetch=1,
        grid=(nb,),
        in_specs=[per_seq(qa), per_seq(qp), per_seq(lat_new), per_seq(kpe_new), per_seq(kinv_new)]
        + [pl.BlockSpec(memory_space=pl.ANY)] * 3,
        out_specs=pl.BlockSpec((1, MLA_HEADS, MLA_KV_RANK), lambda b, pt: (b, 0, 0)),
        scratch_shapes=[
            pltpu.VMEM((2, s, MLA_KV_RANK), F32),
            pltpu.VMEM((2, MLA_ROPE, s), F32),
            pltpu.VMEM((2, MLA_HEADS, s), F32),
            pltpu.SemaphoreType.DMA((3, 2)),
        ],
    )
    return pl.pallas_call(
        functools.partial(_mla_decode_kernel, npages=npages, layer=layer),
        grid_spec=grid_spec,
        out_shape=jax.ShapeDtypeStruct((nb, MLA_HEADS, MLA_KV_RANK), F32),
        compiler_params=_cparams("arbitrary"),
        name="mla_decode",
    )(page_table.reshape(-1), qa, qp, lat_new, kpe_new, kinv_new, lat_pool, kpe_pool, kinv_pool)


def _mla_ctx_out_kernel(ctx_ref, wuv_ref, o_ref):
    for hh in range(MLA_HEADS):
        o_ref[:, hh * MLA_V:(hh + 1) * MLA_V] = _dot(ctx_ref[hh].astype(BF16), wuv_ref[hh]).astype(BF16)


def _mla_ctx_out(ctx_h, wuvh):
    m = ctx_h.shape[1]
    return pl.pallas_call(
        _mla_ctx_out_kernel,
        out_shape=jax.ShapeDtypeStruct((m, MLA_HEADS * MLA_V), BF16),
        name="mla_ctx_out",
    )(ctx_h, wuvh)


def kernel(x_prompt, x_sample, cache_sb_k, cache_sb_v, cache_mla_latent, cache_mla_kpe, cache_mla_kinv,
           state_s5_re, state_s5_im, page_table, ln_mix, ln_mlp, w_up, w_down,
           sb_w_qkv, sb_q_gain, sb_k_gain, sb_w_o,
           s5_a_re, s5_a_im, s5_log_dt, s5_b_re, s5_b_im, s5_c_re, s5_c_im, s5_d, s5_w_glu,
           mla_w_dq, mla_q_ln, mla_w_uq, mla_w_dkv, mla_kv_ln, mla_w_uk, mla_w_uv, mla_q_gain, mla_k_gain,
           mla_w_o):
    nb_p, t, d = x_prompt.shape
    nb_s, t_s, _ = x_sample.shape
    assert t_s == 1 and d == D_MODEL
    npages = page_table.shape[1]
    past_len = npages * PAGE_SIZE
    pos_p = jnp.arange(t, dtype=jnp.int32)
    pos_s = past_len + jnp.arange(t_s, dtype=jnp.int32)
    n_pool = cache_sb_k.shape[1]

    xp = x_prompt.reshape(nb_p * t, d)
    xs = x_sample.reshape(nb_s, d)
    outs = {k: [] for k in ("sbk_p", "sbv_p", "sbk_s", "sbv_s", "lat_p", "kpe_p", "kinv_p",
                            "lat_s", "kpe_s", "kinv_s", "s5r_p", "s5i_p", "s5r_s", "s5i_s")}
    sb_kc = jnp.transpose(cache_sb_k, (0, 1, 3, 4, 2)).reshape(-1, n_pool, SB_NKV, PAGE_SIZE)
    sb_vc = jnp.transpose(cache_sb_v, (0, 1, 3, 4, 2)).reshape(-1, n_pool, SB_NKV, PAGE_SIZE)
    mla_pc = jnp.transpose(cache_mla_kpe, (0, 1, 3, 2))
    mla_ic = jnp.transpose(cache_mla_kinv, (0, 1, 3, 2))
    sb_self_pos = jnp.concatenate([pos_s[-1:], pos_s[-1:]])
    kv_of_head = (jnp.arange(SB_HEADS)[:, None] // SB_GROUP == jnp.arange(SB_KV_HEADS)[None, :])

    for i in range(DEPTH):
        kind, j = i % N_MIXERS, i // N_MIXERS
        g_mix = ln_mix[i][None, :]
        if kind == 0:
            w = sb_w_qkv[j].astype(BF16)
            wo = sb_w_o[j].astype(BF16)
            q, _, _, kt, vt = _sb_proj(xp, g_mix, w, sb_q_gain[j], sb_k_gain[j], nb_p)
            outs["sbk_p"].append(jnp.transpose(kt.reshape(nb_p, SB_KV_HEADS, SB_HEAD_DIM, t), (0, 3, 1, 2)))
            outs["sbv_p"].append(jnp.transpose(vt.reshape(nb_p, SB_KV_HEADS, SB_HEAD_DIM, t), (0, 3, 1, 2)))
            o = _sb_attn(q.reshape(nb_p, t, SB_NQ), kt, vt)
            xp = _proj_res(xp, o.reshape(nb_p * t, SB_NQ), wo)

            q, k, v, _, _ = _sb_proj(xs, g_mix, w, sb_q_gain[j], sb_k_gain[j], 1)
            outs["sbk_s"].append(k.reshape(nb_s, 1, SB_KV_HEADS, SB_HEAD_DIM))
            outs["sbv_s"].append(v.reshape(nb_s, 1, SB_KV_HEADS, SB_HEAD_DIM))
            qbd = (q.reshape(nb_s, SB_HEADS, 1, SB_HEAD_DIM) * kv_of_head[None, :, :, None].astype(BF16))
            qbd = qbd.reshape(nb_s, SB_HEADS, SB_NKV)
            ot = _sb_decode(qbd, k, v, sb_kc, sb_vc, j, page_table, sb_self_pos)
            ot = ot.reshape(nb_s, SB_KV_HEADS, SB_HEAD_DIM, SB_KV_HEADS, SB_GROUP)
            o = jnp.einsum('bkdkg->bkgd', ot).reshape(nb_s, SB_NQ)
            xs = _proj_res(xs, o.astype(BF16), wo)
        elif kind == 1:
            abr, abi, bbr, bbi = _s5_disc(s5_a_re[j], s5_a_im[j], s5_log_dt[j], s5_b_re[j], s5_b_im[j])
            wts = _s5_block_weights(bbr, bbi, s5_c_re[j], s5_c_im[j])
            abr, abi = abr.reshape(1, S5_DIM), abi.reshape(1, S5_DIM)
            dd = s5_d[j][None, :]
            wglu = s5_w_glu[j].astype(BF16)
            zeros = jnp.zeros((nb_p, 1, S5_DIM), F32)
            y, sr, si = _s5(xp.reshape(nb_p, t, d), g_mix, wts, abr, abi, zeros, zeros, dd, wglu, True)
            xp = y.reshape(nb_p * t, d)
            outs["s5r_p"].append(sr.reshape(nb_p, S5_GROUPS, S5_STATE))
            outs["s5i_p"].append(si.reshape(nb_p, S5_GROUPS, S5_STATE))
            h0r = state_s5_re[j].reshape(1, nb_s, S5_DIM)
            h0i = state_s5_im[j].reshape(1, nb_s, S5_DIM)
            y, sr, si = _s5(xs.reshape(1, nb_s, d), g_mix, wts, abr, abi, h0r, h0i, dd, wglu, False)
            xs = y.reshape(nb_s, d)
            outs["s5r_s"].append(sr.reshape(nb_s, S5_GROUPS, S5_STATE))
            outs["s5i_s"].append(si.reshape(nb_s, S5_GROUPS, S5_STATE))
        else:
            mw = _mla_weights(mla_w_dq[j], mla_q_ln[j], mla_w_uq[j], mla_w_dkv[j], mla_kv_ln[j],
                              mla_w_uk[j], mla_w_uv[j], mla_q_gain[j], mla_k_gain[j])
            wo = mla_w_o[j].astype(BF16)
            pe = slice(MLA_NOPE, MLA_QK)
            q, lat, kpet, kinvt, kt, v = _mla_proj(xp, g_mix, mw, pos_p, nb_p)
            outs["lat_p"].append(lat.reshape(nb_p, t, MLA_KV_RANK))
            outs["kpe_p"].append(jnp.transpose(kpet[:, pe, :], (0, 2, 1)))
            outs["kinv_p"].append(jnp.transpose(kinvt, (0, 2, 1)))
            o = _mla_attn(q.reshape(nb_p, t, -1), kt, v.reshape(nb_p, t, -1))
            xp = _proj_res(xp, o.reshape(nb_p * t, -1), wo)

            q, lat, kpet, kinvt, _, _ = _mla_proj(xs, g_mix, mw, jnp.broadcast_to(pos_s, (nb_s,)), 1)
            kpe_s = jnp.transpose(kpet[0, pe, :], (1, 0))
            kinv_s = jnp.transpose(kinvt[0], (1, 0))
            outs["lat_s"].append(lat.reshape(nb_s, 1, MLA_KV_RANK))
            outs["kpe_s"].append(kpe_s.reshape(nb_s, 1, MLA_ROPE))
            outs["kinv_s"].append(kinv_s.reshape(nb_s, 1, MLA_HEADS))
            qa = jnp.transpose(_mla_qabs(q, mw["wukt"]), (1, 0, 2))
            qp = q.reshape(nb_s, MLA_HEADS, MLA_SEG)[:, :, pe]
            ctx = _mla_decode(qa, qp, lat[:, None, :], kpe_s[:, None, :], kinv_s[:, :, None],
                              cache_mla_latent, mla_pc, mla_ic, j, page_table)
            o = _mla_ctx_out(jnp.transpose(ctx, (1, 0, 2)), mw["wuvh"])
            xs = _proj_res(xs, o, wo)
        g_mlp = ln_mlp[i][None, :]
        wu = w_up[i].astype(BF16)
        wd = w_down[i].astype(BF16)
        xp = _mlp(xp, g_mlp, wu, wd)
        xs = _mlp(xs, g_mlp, wu, wd)

    st = lambda k: jnp.stack(outs[k])
    return (xp.reshape(nb_p, t, d), xs.reshape(nb_s, t_s, d),
            st("sbk_p"), st("sbv_p"), st("sbk_s"), st("sbv_s"),
            st("lat_p"), st("kpe_p"), st("kinv_p"), st("lat_s"), st("kpe_s"), st("kinv_s"),
            st("s5r_p"), st("s5i_p"), st("s5r_s"), st("s5i_s"))
```

```python
import functools
import math

import numpy as np
import jax
import jax.numpy as jnp
from jax import lax
from jax.experimental import pallas as pl
from jax.experimental.pallas import tpu as pltpu

F32 = jnp.float32
BF16 = jnp.bfloat16

D_MODEL = 1024
DEPTH = 4
PAGE_SIZE = 128
N_MIXERS = 3
D_FF = 4 * D_MODEL
SB_HEADS = 16
SB_KV_HEADS = 4
SB_HEAD_DIM = 64
SB_GROUP = SB_HEADS // SB_KV_HEADS
SB_SCALE = SB_HEAD_DIM ** -0.5
SB_NQ = SB_HEADS * SB_HEAD_DIM
SB_NKV = SB_KV_HEADS * SB_HEAD_DIM
S5_GROUP_CH = 16
S5_GROUPS = D_MODEL // S5_GROUP_CH
S5_STATE = 64
S5_DIM = S5_GROUPS * S5_STATE
MLA_HEADS = 16
MLA_Q_RANK = 384
MLA_KV_RANK = 256
MLA_NOPE = 64
MLA_ROPE = 32
MLA_V = 64
MLA_QK = MLA_NOPE + MLA_ROPE
MLA_SCALE = MLA_QK ** -0.5
ROPE_THETA = 10000.0
RMS_EPS = 1e-6
NEG_INF = -1e30

LANES = 128
MLA_SEG = LANES
VMEM_LIMIT_BYTES = 56 * 1024 * 1024


def _cparams(*sem):
    return pltpu.CompilerParams(dimension_semantics=sem, vmem_limit_bytes=VMEM_LIMIT_BYTES)


def _rms(x, g):
    xf = x.astype(F32)
    return xf * lax.rsqrt(jnp.mean(xf * xf, axis=-1, keepdims=True) + RMS_EPS) * g


def _dot(a, b):
    return jnp.dot(a, b, preferred_element_type=F32)


def _dot_nt(a, b):
    return lax.dot_general(a, b, (((1,), (1,)), ((), ())), preferred_element_type=F32)


def _split(x):
    hi = x.astype(BF16)
    lo = (x - hi.astype(F32)).astype(BF16)
    return hi, lo


def _sdot(x, s):
    hi, lo = _split(x)
    return _dot(hi, s) + _dot(lo, s)


def _sdot_nt(s, x):
    hi, lo = _split(x)
    return _dot_nt(s, hi) + _dot_nt(s, lo)


def _row_tile(m, want):
    t = min(m, want)
    assert m % t == 0
    return t


def _mlp_kernel(x_ref, g_ref, wu_ref, wd_ref, o_ref, h_ref, acc_ref):
    f = pl.program_id(1)

    @pl.when(f == 0)
    def _():
        h_ref[...] = _rms(x_ref[...], g_ref[...]).astype(BF16)
        acc_ref[...] = jnp.zeros_like(acc_ref)

    a = jnp.maximum(_dot(h_ref[...], wu_ref[...]), 0.0)
    acc_ref[...] += _dot((a * a).astype(BF16), wd_ref[...])

    @pl.when(f == pl.num_programs(1) - 1)
    def _():
        o_ref[...] = x_ref[...] + acc_ref[...]


def _mlp(x, g, wu, wd):
    m = x.shape[0]
    tm = _row_tile(m, 1024)
    tf = 512
    return pl.pallas_call(
        _mlp_kernel,
        grid=(m // tm, D_FF // tf),
        in_specs=[
            pl.BlockSpec((tm, D_MODEL), lambda i, f: (i, 0)),
            pl.BlockSpec((1, D_MODEL), lambda i, f: (0, 0)),
            pl.BlockSpec((D_MODEL, tf), lambda i, f: (0, f)),
            pl.BlockSpec((tf, D_MODEL), lambda i, f: (f, 0)),
        ],
        out_specs=pl.BlockSpec((tm, D_MODEL), lambda i, f: (i, 0)),
        out_shape=jax.ShapeDtypeStruct((m, D_MODEL), F32),
        scratch_shapes=[pltpu.VMEM((tm, D_MODEL), BF16), pltpu.VMEM((tm, D_MODEL), F32)],
        compiler_params=_cparams("parallel", "arbitrary"),
        name="mlp",
    )(x, g, wu, wd)


def _proj_res_kernel(x_ref, o_ref, w_ref, y_ref):
    y_ref[...] = x_ref[...] + _dot(o_ref[...], w_ref[...])


def _proj_res(x, o, w):
    m, k = o.shape
    tm = _row_tile(m, 1024)
    return pl.pallas_call(
        _proj_res_kernel,
        grid=(m // tm,),
        in_specs=[
            pl.BlockSpec((tm, D_MODEL), lambda i: (i, 0)),
            pl.BlockSpec((tm, k), lambda i: (i, 0)),
            pl.BlockSpec((k, D_MODEL), lambda i: (0, 0)),
        ],
        out_specs=pl.BlockSpec((tm, D_MODEL), lambda i: (i, 0)),
        out_shape=jax.ShapeDtypeStruct((m, D_MODEL), F32),
        compiler_params=_cparams("parallel"),
        name="proj_res",
    )(x, o, w)


def _seg_matrix(width, seg, cols=LANES):
    r = np.arange(width)[:, None] // seg
    c = np.arange(cols)[None, :]
    return jnp.asarray((r == c).astype(np.float32), dtype=BF16)


def _sb_proj_kernel(x_ref, g_ref, w_ref, seg_ref, segt_ref, gain_ref, q_ref, k_ref, v_ref, kt_ref, vt_ref):
    h = _rms(x_ref[...], g_ref[...]).astype(BF16)
    qkv = _dot(h, w_ref[...])
    nqk = SB_NQ + SB_NKV
    qk = qkv[:, :nqk]
    ms = _sdot(qk * qk, seg_ref[...]) * (1.0 / SB_HEAD_DIM)
    scale = _sdot(lax.rsqrt(ms + RMS_EPS), segt_ref[...])
    qkn = qk * scale * gain_ref[...]
    kn = qkn[:, SB_NQ:]
    v = qkv[:, nqk:]
    q_ref[...] = qkn[:, :SB_NQ].astype(BF16)
    k_ref[...] = kn
    v_ref[...] = v
    kt_ref[0] = kn.T
    vt_ref[0] = v.T


def _sb_proj(x, g, w, gq, gk, nseq):
    m = x.shape[0]
    t = m // nseq
    tm = _row_tile(t, 512)
    nt = t // tm
    nqk = SB_NQ + SB_NKV
    seg = _seg_matrix(nqk, SB_HEAD_DIM)
    gain = jnp.concatenate([jnp.tile(gq * SB_SCALE, SB_HEADS), jnp.tile(gk, SB_KV_HEADS)])[None, :]
    return pl.pallas_call(
        _sb_proj_kernel,
        grid=(m // tm,),
        in_specs=[
            pl.BlockSpec((tm, D_MODEL), lambda i: (i, 0)),
            pl.BlockSpec((1, D_MODEL), lambda i: (0, 0)),
            pl.BlockSpec(w.shape, lambda i: (0, 0)),
            pl.BlockSpec(seg.shape, lambda i: (0, 0)),
            pl.BlockSpec(seg.T.shape, lambda i: (0, 0)),
            pl.BlockSpec(gain.shape, lambda i: (0, 0)),
        ],
        out_specs=[
            pl.BlockSpec((tm, SB_NQ), lambda i: (i, 0)),
            pl.BlockSpec((tm, SB_NKV), lambda i: (i, 0)),
            pl.BlockSpec((tm, SB_NKV), lambda i: (i, 0)),
            pl.BlockSpec((1, SB_NKV, tm), lambda i: (i // nt, 0, i % nt)),
            pl.BlockSpec((1, SB_NKV, tm), lambda i: (i // nt, 0, i % nt)),
        ],
        out_shape=[
            jax.ShapeDtypeStruct((m, SB_NQ), BF16),
            jax.ShapeDtypeStruct((m, SB_NKV), F32),
            jax.ShapeDtypeStruct((m, SB_NKV), F32),
            jax.ShapeDtypeStruct((nseq, SB_NKV, t), F32),
            jax.ShapeDtypeStruct((nseq, SB_NKV, t), F32),
        ],
        compiler_params=_cparams("parallel"),
        name="sb_proj",
    )(x, g, w, seg, seg.T, gain)


def _tri_ones():
    j = np.arange(LANES)[:, None]
    s = np.arange(LANES)[None, :]
    u = np.concatenate([(j > s).astype(np.float32), np.ones((LANES, LANES), np.float32)], axis=1)
    return jnp.asarray(np.concatenate([u, u], axis=0), dtype=BF16)


def _log_sigmoid_neg(z):
    return jnp.minimum(-z, 0.0) - jnp.log(1.0 + jnp.exp(-jnp.abs(z)))


SB_CUM = 2 * LANES
LOG2E = math.log2(math.e)


def _tri_upper():
    j = np.arange(SB_CUM)[:, None]
    s = np.arange(SB_CUM)[None, :]
    return jnp.asarray((j > s).astype(np.float32), dtype=BF16)


def _sb_attn_kernel(qi_ref, kv_ref, q_ref, kt_ref, vt_ref, u_ref, o_ref, acc_ref, c_ref, *, tq, tk):
    s = pl.program_id(2)
    i = qi_ref[s]
    kv = kv_ref[s]
    r = tq // tk
    first = kv == (i + 1) * r - 1
    last = kv == 0

    @pl.when(first)
    def _():
        acc_ref[...] = jnp.zeros_like(acc_ref)
        c_ref[...] = jnp.zeros_like(c_ref)

    def step(masked):
        kt = kt_ref[0].astype(BF16)
        vt = vt_ref[0].astype(BF16)
        u = u_ref[...]
        if masked:
            row = lax.broadcasted_iota(jnp.int32, (tq, tk), 0) + i * tq
            col = lax.broadcasted_iota(jnp.int32, (tq, tk), 1) + kv * tk
            valid = col < row
        for g in range(SB_GROUP):
            qg = q_ref[0, :, g * SB_HEAD_DIM:(g + 1) * SB_HEAD_DIM]
            z = _dot(qg, kt)
            sp = jnp.maximum(z, 0.0) + jnp.log(1.0 + jnp.exp2(jnp.abs(z) * (-LOG2E)))
            if masked:
                sp = jnp.where(valid, sp, 0.0)
            spb = sp.astype(BF16)
            c = c_ref[g]
            nsub = tk // SB_CUM
            tails = [None] * nsub
            for sb in reversed(range(nsub)):
                blk = spb[:, sb * SB_CUM:(sb + 1) * SB_CUM]
                later = _dot(blk, u)
                tails[sb] = later + jnp.concatenate([c, c], axis=1)
                c = c + (later[:, :1] + blk[:, :1].astype(F32))
            c_ref[g] = c
            w = jnp.exp(z - sp - jnp.concatenate(tails, axis=1))
            if masked:
                w = jnp.where(valid, w, 0.0)
            acc_ref[g] += _dot_nt(w.astype(BF16), vt)

    diag = kv >= i * r

    @pl.when(diag)
    def _():
        step(True)

    @pl.when(jnp.logical_not(diag))
    def _():
        step(False)

    @pl.when(last)
    def _():
        o_ref[0] = jnp.concatenate([acc_ref[g] for g in range(SB_GROUP)], axis=1).astype(BF16)


def _causal_pairs(nq, r, reverse):
    qi, kv = [], []
    for i in range(nq):
        ks = list(range((i + 1) * r))
        if reverse:
            ks = ks[::-1]
        qi += [i] * len(ks)
        kv += ks
    return jnp.asarray(qi, jnp.int32), jnp.asarray(kv, jnp.int32)


def _sb_attn(q, kt, vt):
    b, t, _ = q.shape
    tq = _row_tile(t, 512)
    tk = tq
    qi, kv = _causal_pairs(t // tq, tq // tk, reverse=True)
    gw = SB_GROUP * SB_HEAD_DIM
    uo = _tri_upper()
    grid_spec = pltpu.PrefetchScalarGridSpec(
        num_scalar_prefetch=2,
        grid=(b, SB_KV_HEADS, int(qi.shape[0])),
        in_specs=[
            pl.BlockSpec((1, tq, gw), lambda bb, h, s, qi_r, kv_r: (bb, qi_r[s], h)),
            pl.BlockSpec((1, SB_HEAD_DIM, tk), lambda bb, h, s, qi_r, kv_r: (bb, h, kv_r[s])),
            pl.BlockSpec((1, SB_HEAD_DIM, tk), lambda bb, h, s, qi_r, kv_r: (bb, h, kv_r[s])),
            pl.BlockSpec(uo.shape, lambda bb, h, s, qi_r, kv_r: (0, 0)),
        ],
        out_specs=pl.BlockSpec((1, tq, gw), lambda bb, h, s, qi_r, kv_r: (bb, qi_r[s], h)),
        scratch_shapes=[
            pltpu.VMEM((SB_GROUP, tq, SB_HEAD_DIM), F32),
            pltpu.VMEM((SB_GROUP, tq, LANES), F32),
        ],
    )
    return pl.pallas_call(
        functools.partial(_sb_attn_kernel, tq=tq, tk=tk),
        grid_spec=grid_spec,
        out_shape=jax.ShapeDtypeStruct((b, t, SB_NQ), BF16),
        compiler_params=_cparams("parallel", "parallel", "arbitrary"),
        name="sb_attn",
    )(qi, kv, q, kt, vt, uo)


def _page_copies(pt_ref, seq, slot, *, npages, layer, pools, bufs, sem):
    out = []
    for n, (pool, buf) in enumerate(zip(pools, bufs)):
        for p in range(npages):
            pg = pt_ref[seq * npages + p]
            out.append(pltpu.make_async_copy(
                pool.at[layer, pg], buf.at[slot, :, pl.ds(p * PAGE_SIZE, PAGE_SIZE)], sem.at[n, slot]))
    return out


def _sb_decode_kernel(pt_ref, vs_ref, qbd_ref, qbdt_ref, knr_ref, knc_ref, vnc_ref, uo_ref, kc_ref, vc_ref,
                      o_ref, kbuf, vbuf, sem, *, npages, layer):
    b = pl.program_id(0)
    nb = pl.num_programs(0)
    slot = b % 2
    copies = functools.partial(_page_copies, pt_ref, npages=npages, layer=layer, pools=(kc_ref, vc_ref),
                               bufs=(kbuf, vbuf), sem=sem)

    @pl.when(b == 0)
    def _():
        for cp in copies(0, 0):
            cp.start()

    @pl.when(b + 1 < nb)
    def _():
        for cp in copies(b + 1, 1 - slot):
            cp.start()

    for cp in copies(b, slot):
        cp.wait()

    qbd = qbd_ref[0]
    nh = qbd.shape[0]
    self_valid = vs_ref[0] < vs_ref[1]
    z_col = jnp.sum(qbd.astype(F32) * knr_ref[0], axis=1, keepdims=True)
    l_col = jnp.where(self_valid, _log_sigmoid_neg(z_col), 0.0)
    z_row = jnp.sum(qbdt_ref[0].astype(F32) * knc_ref[0], axis=0, keepdims=True)
    w_row = jnp.where(self_valid, jnp.exp(z_row + _log_sigmoid_neg(z_row)), 0.0)

    z_all = _dot(qbd, kbuf[slot])
    z = jnp.concatenate([z_all[:, p * PAGE_SIZE:(p + 1) * PAGE_SIZE] for p in range(npages)], axis=0)
    lneg = _log_sigmoid_neg(z)
    hi, lo = _split(lneg)
    res = _dot(jnp.concatenate([hi, lo], axis=1), uo_ref[...])
    c = jnp.broadcast_to(l_col, (nh, PAGE_SIZE))
    tails = [None] * npages
    for p in reversed(range(npages)):
        rs = slice(p * nh, (p + 1) * nh)
        tails[p] = res[rs, :PAGE_SIZE] + c
        c = c + res[rs, PAGE_SIZE:]
    w = jnp.exp(z + lneg + jnp.concatenate(tails, axis=0))
    w_all = jnp.concatenate([w[p * nh:(p + 1) * nh] for p in range(npages)], axis=1)
    o_ref[0] = _dot_nt(vbuf[slot], w_all) + vnc_ref[0] * w_row


def _sb_decode(qbd, k_new, v_new, kc, vc, layer, page_table, self_pos):
    nb, nh, kw = qbd.shape
    npages = page_table.shape[1]
    s = npages * PAGE_SIZE
    uo = _tri_ones()
    im = lambda b, pt, vs: (b, 0, 0)
    grid_spec = pltpu.PrefetchScalarGridSpec(
        num_scalar_prefetch=2,
        grid=(nb,),
        in_specs=[
            pl.BlockSpec((1, nh, kw), im),
            pl.BlockSpec((1, kw, nh), im),
            pl.BlockSpec((1, 1, kw), im),
            pl.BlockSpec((1, kw, 1), im),
            pl.BlockSpec((1, kw, 1), im),
            pl.BlockSpec(uo.shape, lambda b, pt, vs: (0, 0)),
            pl.BlockSpec(memory_space=pl.ANY),
            pl.BlockSpec(memory_space=pl.ANY),
        ],
        out_specs=pl.BlockSpec((1, kw, nh), im),
        scratch_shapes=[
            pltpu.VMEM((2, kw, s), F32),
            pltpu.VMEM((2, kw, s), F32),
            pltpu.SemaphoreType.DMA((2, 2)),
        ],
    )
    return pl.pallas_call(
        functools.partial(_sb_decode_kernel, npages=npages, layer=layer),
        grid_spec=grid_spec,
        out_shape=jax.ShapeDtypeStruct((nb, kw, nh), F32),
        compiler_params=_cparams("arbitrary"),
        name="sb_decode",
    )(page_table.reshape(-1), self_pos, qbd, jnp.transpose(qbd, (0, 2, 1)), k_new[:, None, :],
      k_new[:, :, None], v_new[:, :, None], uo, kc, vc)


def _s5_disc_kernel(are_ref, aim_ref, ldt_ref, arex_ref, aimx_ref, ldtx_ref, bre_ref, bim_ref,
                    abr_ref, abi_ref, bbr_ref, bbi_ref):
    def zoh(a_re, a_im, log_dt):
        dt = jnp.exp(log_dt)
        mag = jnp.exp(dt * a_re)
        ab_re = mag * jnp.cos(dt * a_im)
        ab_im = mag * jnp.sin(dt * a_im)
        den = a_re * a_re + a_im * a_im
        xr, yi = ab_re - 1.0, ab_im
        return ab_re, ab_im, (xr * a_re + yi * a_im) / den, (yi * a_re - xr * a_im) / den

    ab_re, ab_im, _, _ = zoh(are_ref[...], aim_ref[...], ldt_ref[...])
    abr_ref[...] = ab_re
    abi_ref[...] = ab_im
    _, _, c_re, c_im = zoh(arex_ref[...], aimx_ref[...], ldtx_ref[...])
    b_re = bre_ref[...]
    b_im = bim_ref[...]
    bbr_ref[...] = c_re * b_re - c_im * b_im
    bbi_ref[...] = c_re * b_im + c_im * b_re


def _s5_disc(a_re, a_im, log_dt, b_re, b_im):
    g, p = a_re.shape
    c = b_re.shape[-1]
    rep = lambda a: jnp.repeat(a, c, axis=1)
    ldt = jnp.broadcast_to(log_dt[:, None], (g, p))
    args = (a_re, a_im, ldt, rep(a_re), rep(a_im), rep(ldt), b_re.reshape(g, p * c), b_im.reshape(g, p * c))
    small = jax.ShapeDtypeStruct((g, p), F32)
    wide = jax.ShapeDtypeStruct((g, p * c), F32)
    return pl.pallas_call(
        _s5_disc_kernel,
        out_shape=[small, small, wide, wide],
        name="s5_disc",
    )(*args)


def _s5_block_weights(bb_re, bb_im, c_re, c_im):
    g, c, p = S5_GROUPS, S5_GROUP_CH, S5_STATE
    nblk = 4
    gl = g // nblk
    eye = jnp.eye(gl, dtype=F32)

    def in_blk(bb):
        bb = bb.reshape(nblk, gl, p, c)
        return jnp.einsum('kgpc,gh->kgchp', bb, eye).reshape(nblk, gl * c, gl * p).astype(BF16)

    def out_blk(cc):
        cc = cc.reshape(nblk, gl, c, p)
        return jnp.einsum('kgcp,gh->kgphc', cc, eye).reshape(nblk, gl * p, gl * c).astype(BF16)

    return in_blk(bb_re), in_blk(bb_im), out_blk(c_re), out_blk(c_im)


def _s5_kernel(x_ref, g_ref, bre_ref, bim_ref, abr_ref, abi_ref, h0r_ref, h0i_ref, cre_ref, cim_ref,
               d_ref, wg_ref, o_ref, sr_ref, si_ref, str_ref, sti_ref, hr_ref, hi_ref, *, sequential):
    nblk = bre_ref.shape[0]
    cw = D_MODEL // nblk
    sw = S5_DIM // nblk
    x = x_ref[0]
    tt = x.shape[0]
    u = _rms(x, g_ref[...])
    ub = u.astype(BF16)
    for k in range(nblk):
        uk = ub[:, k * cw:(k + 1) * cw]
        str_ref[:, k * sw:(k + 1) * sw] = _dot(uk, bre_ref[k])
        sti_ref[:, k * sw:(k + 1) * sw] = _dot(uk, bim_ref[k])

    if sequential:
        @pl.when(pl.program_id(1) == 0)
        def _():
            hr_ref[...] = h0r_ref[0]
            hi_ref[...] = h0i_ref[0]

        for k in range(nblk):
            cs = slice(k * sw, (k + 1) * sw)
            ar = abr_ref[:, cs]
            ai = abi_ref[:, cs]

            def body(r, carry):
                hr, hi = carry
                br = str_ref[pl.ds(r, 1), cs]
                bi = sti_ref[pl.ds(r, 1), cs]
                nr = ar * hr - ai * hi + br
                ni = ar * hi + ai * hr + bi
                str_ref[pl.ds(r, 1), cs] = nr
                sti_ref[pl.ds(r, 1), cs] = ni
                return nr, ni

            hr, hi = lax.fori_loop(0, tt, body, (hr_ref[:, cs], hi_ref[:, cs]))
            hr_ref[:, cs] = hr
            hi_ref[:, cs] = hi
        sr_ref[0] = hr_ref[...]
        si_ref[0] = hi_ref[...]
    else:
        ar = abr_ref[...]
        ai = abi_ref[...]
        h0r = h0r_ref[0]
        h0i = h0i_ref[0]
        nr = ar * h0r - ai * h0i + str_ref[...]
        ni = ar * h0i + ai * h0r + sti_ref[...]
        str_ref[...] = nr
        sti_ref[...] = ni
        sr_ref[0] = nr
        si_ref[0] = ni

    ys = []
    for k in range(nblk):
        cs = slice(k * sw, (k + 1) * sw)
        ys.append(_dot(str_ref[:, cs].astype(BF16), cre_ref[k]) - _dot(sti_ref[:, cs].astype(BF16), cim_ref[k]))
    y = jnp.concatenate(ys, axis=1) + d_ref[...] * u
    y = jax.nn.gelu(y)
    vg = _dot(y.astype(BF16), wg_ref[...])
    o_ref[0] = x + vg[:, :D_MODEL] * jax.nn.sigmoid(vg[:, D_MODEL:])


def _s5(x, g, wts, abr, abi, h0r, h0i, d, wglu, sequential):
    b, t, _ = x.shape
    tt = _row_tile(t, 256)
    bre, bim, cre, cim = wts
    srow = 1 if sequential else tt
    st_spec = pl.BlockSpec((1, srow, S5_DIM), (lambda bb, i: (bb, 0, 0)) if sequential else (lambda bb, i: (bb, i, 0)))
    full = lambda a: pl.BlockSpec(a.shape, lambda bb, i: (0,) * a.ndim)
    return pl.pallas_call(
        functools.partial(_s5_kernel, sequential=sequential),
        grid=(b, t // tt),
        in_specs=[
            pl.BlockSpec((1, tt, D_MODEL), lambda bb, i: (bb, i, 0)),
            full(g), full(bre), full(bim), full(abr), full(abi),
            st_spec, st_spec,
            full(cre), full(cim), full(d), full(wglu),
        ],
        out_specs=[pl.BlockSpec((1, tt, D_MODEL), lambda bb, i: (bb, i, 0)), st_spec, st_spec],
        out_shape=[
            jax.ShapeDtypeStruct((b, t, D_MODEL), F32),
            jax.ShapeDtypeStruct((b, t if not sequential else 1, S5_DIM), F32),
            jax.ShapeDtypeStruct((b, t if not sequential else 1, S5_DIM), F32),
        ],
        scratch_shapes=[
            pltpu.VMEM((tt, S5_DIM), F32), pltpu.VMEM((tt, S5_DIM), F32),
            pltpu.VMEM((1, S5_DIM), F32), pltpu.VMEM((1, S5_DIM), F32),
        ],
        compiler_params=_cparams("parallel", "arbitrary"),
        name="s5_seq" if sequential else "s5_step",
    )(x, g, bre, bim, abr, abi, h0r, h0i, cre, cim, d, wglu)


def _seg_cols(w_nope, w_pe):
    k, h, _ = w_nope.shape
    w_pe = jnp.broadcast_to(w_pe, (k, h, MLA_ROPE))
    pad = jnp.zeros((k, h, MLA_SEG - MLA_QK), w_nope.dtype)
    return jnp.concatenate([w_nope, w_pe, pad], axis=-1).reshape(k, h * MLA_SEG)


def _seg_vec(v_nope, v_pe, fill=0.0):
    pad = jnp.full((MLA_SEG - MLA_QK,), fill, F32)
    return jnp.concatenate([v_nope, v_pe, pad])[None, :]


def _rope_tables(pos):
    half = MLA_ROPE // 2
    freqs = ROPE_THETA ** (-jnp.arange(half, dtype=F32) / half)
    ang = pos.astype(F32)[:, None] * freqs[None, :]
    cos, sin = jnp.cos(ang), jnp.sin(ang)
    n = pos.shape[0]
    one = jnp.ones((n, MLA_NOPE), F32)
    zero = jnp.zeros((n, MLA_NOPE), F32)
    z32 = jnp.zeros((n, MLA_SEG - MLA_QK), F32)
    z16 = jnp.zeros((n, half), F32)
    c = jnp.concatenate([one, cos, cos, z32], axis=1)
    s_lo = jnp.concatenate([zero, -sin, z16, z32], axis=1)
    s_hi = jnp.concatenate([zero, z16, sin, z32], axis=1)
    return c, s_lo, s_hi


def _rope_seg(x, c, s_lo, s_hi):
    half = MLA_ROPE // 2
    up = pltpu.roll(x, MLA_SEG - half, 1)
    dn = pltpu.roll(x, half, 1)
    return x * c + up * s_lo + dn * s_hi


def _mla_proj_kernel(x_ref, g_ref, wdq_ref, gql_ref, wuq_ref, gq_ref, gkq_ref, wdkv_ref, gkvl_ref,
                     wuk_ref, wuv_ref, vone_ref, gkpe_ref, segt_ref, c_ref, slo_ref, shi_ref,
                     q_ref, lat_ref, kpe_ref, kinvt_ref, kt_ref, v_ref):
    h = _rms(x_ref[...], g_ref[...]).astype(BF16)
    c, s_lo, s_hi = c_ref[...], slo_ref[...], shi_ref[...]
    cq = _rms(_dot(h, wdq_ref[...]), gql_ref[...]).astype(BF16)
    qraw = _dot(cq, wuq_ref[...])
    for hh in range(MLA_HEADS):
        qs = qraw[:, hh * MLA_SEG:(hh + 1) * MLA_SEG]
        qs = qs * lax.rsqrt(jnp.sum(qs * qs, axis=1, keepdims=True) * (1.0 / MLA_QK) + RMS_EPS) * gq_ref[...]
        qs = _rope_seg(qs * gkq_ref[...], c, s_lo, s_hi)
        q_ref[:, hh * MLA_SEG:(hh + 1) * MLA_SEG] = qs.astype(BF16)
    ckv = _dot(h, wdkv_ref[...])
    lat = _rms(ckv[:, :MLA_KV_RANK], gkvl_ref[...])
    lat_ref[...] = lat
    latb = lat.astype(BF16)
    kpe_raw = ckv[:, MLA_KV_RANK:]
    kn = _dot(latb, wuk_ref[...])
    ones = jnp.ones((8, MLA_SEG), BF16)
    pe_ss = _sdot_nt(ones, kpe_raw * kpe_raw)[:1]
    ms = (_sdot_nt(segt_ref[...], kn * kn) + pe_ss) * (1.0 / MLA_QK)
    kinvt = lax.rsqrt(ms + RMS_EPS)
    kinvt_ref[0] = kinvt
    kpe = _rope_seg(kpe_raw * gkpe_ref[...], c, s_lo, s_hi)
    kpe_ref[0] = kpe.T
    for hh in range(MLA_HEADS):
        ks = (kn[:, hh * MLA_SEG:(hh + 1) * MLA_SEG] + kpe).T
        kt_ref[0, hh] = (ks * (kinvt[hh:hh + 1] * (MLA_SCALE * LOG2E))).astype(BF16)
    v_ref[...] = (_dot(latb, wuv_ref[...]) + vone_ref[...]).astype(BF16)


def _mla_weights(w_dq, g_ql, w_uq, w_dkv, g_kvl, w_uk, w_uv, g_q, g_k):
    half = MLA_ROPE // 2
    uq = w_uq.reshape(MLA_Q_RANK, MLA_HEADS, MLA_QK)
    wuq = _seg_cols(uq[..., :MLA_NOPE], uq[..., MLA_NOPE:]).astype(BF16)
    kvn = jnp.zeros((D_MODEL, 1, MLA_NOPE), F32)
    wdkv = jnp.concatenate([w_dkv[:, :MLA_KV_RANK], _seg_cols(kvn, w_dkv[:, None, MLA_KV_RANK:])], axis=1).astype(BF16)
    wuk = _seg_cols(w_uk, jnp.zeros((MLA_KV_RANK, 1, MLA_ROPE), F32)).astype(BF16)
    return dict(
        wdq=w_dq.astype(BF16), gql=g_ql[None, :], wuq=wuq,
        gq=_seg_vec(g_q[:MLA_NOPE], g_q[MLA_NOPE:]),
        gkq=_seg_vec(g_k[:MLA_NOPE], jnp.ones((MLA_ROPE,), F32)),
        wdkv=wdkv, gkvl=g_kvl[None, :], wuk=wuk,
        wuv=jnp.concatenate([w_uv, jnp.zeros_like(w_uv)], axis=-1).reshape(MLA_KV_RANK, -1).astype(BF16),
        vone=jnp.tile(jnp.concatenate([jnp.zeros((MLA_V,), F32), jnp.ones((MLA_SEG - MLA_V,), F32)]),
                      MLA_HEADS)[None, :],
        gkpe=_seg_vec(jnp.zeros((MLA_NOPE,), F32), g_k[MLA_NOPE:]),
        segt=_seg_matrix(MLA_HEADS * MLA_SEG, MLA_SEG, cols=MLA_HEADS).T,
        wukt=jnp.transpose(w_uk, (1, 2, 0)).astype(BF16),
        wuvh=jnp.transpose(w_uv, (1, 0, 2)).astype(BF16),
    )


def _mla_proj(x, g, mw, pos, nseq):
    m = x.shape[0]
    t = m // nseq
    tm = _row_tile(t, 256)
    nt = t // tm
    tabs = _rope_tables(pos)
    hs = MLA_HEADS * MLA_SEG
    full = lambda a: pl.BlockSpec(a.shape, lambda i: (0,) * a.ndim)
    tab_spec = pl.BlockSpec((tm, MLA_SEG), lambda i: (i % nt, 0))
    tmaj = lambda rows: pl.BlockSpec((1, rows, tm), lambda i: (i // nt, 0, i % nt))
    ws = [mw[k] for k in ("wdq", "gql", "wuq", "gq", "gkq", "wdkv", "gkvl", "wuk", "wuv", "vone", "gkpe", "segt")]
    return pl.pallas_call(
        _mla_proj_kernel,
        grid=(m // tm,),
        in_specs=[pl.BlockSpec((tm, D_MODEL), lambda i: (i, 0)), full(g)] + [full(w) for w in ws]
        + [tab_spec] * 3,
        out_specs=[
            pl.BlockSpec((tm, hs), lambda i: (i, 0)),
            pl.BlockSpec((tm, MLA_KV_RANK), lambda i: (i, 0)),
            tmaj(MLA_SEG),
            tmaj(MLA_HEADS),
            pl.BlockSpec((1, MLA_HEADS, MLA_SEG, tm), lambda i: (i // nt, 0, 0, i % nt)),
            pl.BlockSpec((tm, hs), lambda i: (i, 0)),
        ],
        out_shape=[
            jax.ShapeDtypeStruct((m, hs), BF16),
            jax.ShapeDtypeStruct((m, MLA_KV_RANK), F32),
            jax.ShapeDtypeStruct((nseq, MLA_SEG, t), F32),
            jax.ShapeDtypeStruct((nseq, MLA_HEADS, t), F32),
            jax.ShapeDtypeStruct((nseq, MLA_HEADS, MLA_SEG, t), BF16),
            jax.ShapeDtypeStruct((m, hs), BF16),
        ],
        compiler_params=_cparams("parallel"),
        name="mla_proj",
    )(x, g, *ws, *tabs)


MLA_HPS = 4


def _mla_attn_kernel(qi_ref, kv_ref, q_ref, kt_ref, v_ref, o_ref, m_ref, acc_ref, *, tq, tk):
    s = pl.program_id(2)
    i = qi_ref[s]
    kv = kv_ref[s]
    r = tq // tk

    @pl.when(kv == 0)
    def _():
        m_ref[...] = jnp.full_like(m_ref, NEG_INF)
        acc_ref[...] = jnp.zeros_like(acc_ref)

    def step(masked):
        if masked:
            row = lax.broadcasted_iota(jnp.int32, (tq, tk), 0) + i * tq
            col = lax.broadcasted_iota(jnp.int32, (tq, tk), 1) + kv * tk
            valid = col <= row
        for hh in range(MLA_HPS):
            sc = _dot(q_ref[0, :, hh * MLA_SEG:(hh + 1) * MLA_SEG], kt_ref[0, hh])
            if masked:
                sc = jnp.where(valid, sc, NEG_INF)
            m_prev = m_ref[hh]
            m_new = jnp.maximum(m_prev, jnp.max(sc, axis=1, keepdims=True))
            p = jnp.exp2(sc - m_new[:, :1])
            pv = _dot(p.astype(BF16), v_ref[0, :, hh * MLA_SEG:(hh + 1) * MLA_SEG])
            acc_ref[hh] = acc_ref[hh] * jnp.exp2(m_prev - m_new) + pv
            m_ref[hh] = m_new

    diag = kv >= i * r

    @pl.when(diag)
    def _():
        step(True)

    @pl.when(jnp.logical_not(diag))
    def _():
        step(False)

    @pl.when(kv == (i + 1) * r - 1)
    def _():
        outs = []
        for hh in range(MLA_HPS):
            a = acc_ref[hh]
            outs.append((a / pltpu.roll(a, MLA_V, 1))[:, :MLA_V])
        o_ref[0] = jnp.concatenate(outs, axis=1).astype(BF16)


def _mla_attn(q, kt, v):
    b, t, _ = q.shape
    tq = _row_tile(t, 512)
    tk = tq
    qi, kvt = _causal_pairs(t // tq, tq // tk, reverse=False)
    grid_spec = pltpu.PrefetchScalarGridSpec(
        num_scalar_prefetch=2,
        grid=(b, MLA_HEADS // MLA_HPS, int(qi.shape[0])),
        in_specs=[
            pl.BlockSpec((1, tq, MLA_HPS * MLA_SEG), lambda bb, h, s, qi_r, kv_r: (bb, qi_r[s], h)),
            pl.BlockSpec((1, MLA_HPS, MLA_SEG, tk), lambda bb, h, s, qi_r, kv_r: (bb, h, 0, kv_r[s])),
            pl.BlockSpec((1, tk, MLA_HPS * MLA_SEG), lambda bb, h, s, qi_r, kv_r: (bb, kv_r[s], h)),
        ],
        out_specs=pl.BlockSpec((1, tq, MLA_HPS * MLA_V), lambda bb, h, s, qi_r, kv_r: (bb, qi_r[s], h)),
        scratch_shapes=[
            pltpu.VMEM((MLA_HPS, tq, LANES), F32),
            pltpu.VMEM((MLA_HPS, tq, LANES), F32),
        ],
    )
    return pl.pallas_call(
        functools.partial(_mla_attn_kernel, tq=tq, tk=tk),
        grid_spec=grid_spec,
        out_shape=jax.ShapeDtypeStruct((b, t, MLA_HEADS * MLA_V), BF16),
        compiler_params=_cparams("parallel", "parallel", "arbitrary"),
        name="mla_attn",
    )(qi, kvt, q, kt, v)


def _mla_qabs_kernel(q_ref, wukt_ref, qa_ref):
    for hh in range(MLA_HEADS):
        qn = q_ref[:, hh * MLA_SEG:hh * MLA_SEG + MLA_NOPE]
        qa_ref[hh] = _dot(qn, wukt_ref[hh])


def _mla_qabs(q, wukt):
    m = q.shape[0]
    return pl.pallas_call(
        _mla_qabs_kernel,
        out_shape=jax.ShapeDtypeStruct((MLA_HEADS, m, MLA_KV_RANK), F32),
        name="mla_qabs",
    )(q, wukt)


def _mla_decode_kernel(pt_ref, qa_ref, qp_ref, latn_ref, kpen_ref, kinvn_ref, lc_ref, pc_ref, ic_ref,
                       o_ref, lbuf, pbuf, ibuf, sem, *, npages, layer):
    b = pl.program_id(0)
    nb = pl.num_programs(0)
    slot = b % 2

    def copies(seq, sl):
        out = []
        for p in range(npages):
            pg = pt_ref[seq * npages + p]
            rows = pl.ds(p * PAGE_SIZE, PAGE_SIZE)
            out.append(pltpu.make_async_copy(lc_ref.at[layer, pg], lbuf.at[sl, rows, :], sem.at[0, sl]))
            out.append(pltpu.make_async_copy(pc_ref.at[layer, pg], pbuf.at[sl, :, rows], sem.at[1, sl]))
            out.append(pltpu.make_async_copy(ic_ref.at[layer, pg], ibuf.at[sl, :, rows], sem.at[2, sl]))
        return out

    @pl.when(b == 0)
    def _():
        for cp in copies(0, 0):
            cp.start()

    @pl.when(b + 1 < nb)
    def _():
        for cp in copies(b + 1, 1 - slot):
            cp.start()

    for cp in copies(b, slot):
        cp.wait()

    qa = qa_ref[0].astype(BF16)
    qp = qp_ref[0]
    lat = lbuf[slot]
    sc = (_dot_nt(qa, lat) + _dot(qp, pbuf[slot])) * MLA_SCALE * ibuf[slot]
    latn = latn_ref[0]
    rnd = lambda a: a.astype(BF16).astype(F32)
    s_self = (jnp.sum(rnd(qa) * rnd(latn), axis=1, keepdims=True)
              + jnp.sum(rnd(qp) * rnd(kpen_ref[0]), axis=1, keepdims=True))
    s_self = s_self * MLA_SCALE * kinvn_ref[0]
    m = jnp.maximum(jnp.max(sc, axis=1, keepdims=True), s_self)
    e = jnp.exp(sc - m)
    e_self = jnp.exp(s_self - m)
    den = jnp.sum(e, axis=1, keepdims=True) + e_self
    ctx = _dot(e, lat) + e_self * latn
    o_ref[0] = ctx / den


def _mla_decode(qa, qp, lat_new, kpe_new, kinv_new, lat_pool, kpe_pool, kinv_pool, layer, page_table):
    nb = qa.shape[0]
    npages = page_table.shape[1]
    s = npages * PAGE_SIZE
    per_seq = lambda a: pl.BlockSpec((1,) + a.shape[1:], lambda b, pt: (b, 0, 0))
    grid_spec = pltpu.PrefetchScalarGridSpec(
        num_scalar_prefetch=1,
        grid=(nb,),
        in_specs=[per_seq(qa), per_seq(qp), per_seq(lat_new), per_seq(kpe_new), per_seq(kinv_new)]
        + [pl.BlockSpec(memory_space=pl.ANY)] * 3,
        out_specs=pl.BlockSpec((1, MLA_HEADS, MLA_KV_RANK), lambda b, pt: (b, 0, 0)),
        scratch_shapes=[
            pltpu.VMEM((2, s, MLA_KV_RANK), F32),
            pltpu.VMEM((2, MLA_ROPE, s), F32),
            pltpu.VMEM((2, MLA_HEADS, s), F32),
            pltpu.SemaphoreType.DMA((3, 2)),
        ],
    )
    return pl.pallas_call(
        functools.partial(_mla_decode_kernel, npages=npages, layer=layer),
        grid_spec=grid_spec,
        out_shape=jax.ShapeDtypeStruct((nb, MLA_HEADS, MLA_KV_RANK), F32),
        compiler_params=_cparams("arbitrary"),
        name="mla_decode",
    )(page_table.reshape(-1), qa, qp, lat_new, kpe_new, kinv_new, lat_pool, kpe_pool, kinv_pool)


def _mla_ctx_out_kernel(ctx_ref, wuv_ref, o_ref):
    for hh in range(MLA_HEADS):
        o_ref[:, hh * MLA_V:(hh + 1) * MLA_V] = _dot(ctx_ref[hh].astype(BF16), wuv_ref[hh]).astype(BF16)


def _mla_ctx_out(ctx_h, wuvh):
    m = ctx_h.shape[1]
    return pl.pallas_call(
        _mla_ctx_out_kernel,
        out_shape=jax.ShapeDtypeStruct((m, MLA_HEADS * MLA_V), BF16),
        name="mla_ctx_out",
    )(ctx_h, wuvh)


def kernel(x_prompt, x_sample, cache_sb_k, cache_sb_v, cache_mla_latent, cache_mla_kpe, cache_mla_kinv,
           state_s5_re, state_s5_im, page_table, ln_mix, ln_mlp, w_up, w_down,
           sb_w_qkv, sb_q_gain, sb_k_gain, sb_w_o,
           s5_a_re, s5_a_im, s5_log_dt, s5_b_re, s5_b_im, s5_c_re, s5_c_im, s5_d, s5_w_glu,
           mla_w_dq, mla_q_ln, mla_w_uq, mla_w_dkv, mla_kv_ln, mla_w_uk, mla_w_uv, mla_q_gain, mla_k_gain,
           mla_w_o):
    nb_p, t, d = x_prompt.shape
    nb_s, t_s, _ = x_sample.shape
    assert t_s == 1 and d == D_MODEL
    npages = page_table.shape[1]
    past_len = npages * PAGE_SIZE
    pos_p = jnp.arange(t, dtype=jnp.int32)
    pos_s = past_len + jnp.arange(t_s, dtype=jnp.int32)
    n_pool = cache_sb_k.shape[1]

    xp = x_prompt.reshape(nb_p * t, d)
    xs = x_sample.reshape(nb_s, d)
    outs = {k: [] for k in ("sbk_p", "sbv_p", "sbk_s", "sbv_s", "lat_p", "kpe_p", "kinv_p",
                            "lat_s", "kpe_s", "kinv_s", "s5r_p", "s5i_p", "s5r_s", "s5i_s")}
    sb_kc = jnp.transpose(cache_sb_k, (0, 1, 3, 4, 2)).reshape(-1, n_pool, SB_NKV, PAGE_SIZE)
    sb_vc = jnp.transpose(cache_sb_v, (0, 1, 3, 4, 2)).reshape(-1, n_pool, SB_NKV, PAGE_SIZE)
    mla_pc = jnp.transpose(cache_mla_kpe, (0, 1, 3, 2))
    mla_ic = jnp.transpose(cache_mla_kinv, (0, 1, 3, 2))
    sb_self_pos = jnp.concatenate([pos_s[-1:], pos_s[-1:]])
    kv_of_head = (jnp.arange(SB_HEADS)[:, None] // SB_GROUP == jnp.arange(SB_KV_HEADS)[None, :])

    for i in range(DEPTH):
        kind, j = i % N_MIXERS, i // N_MIXERS
        g_mix = ln_mix[i][None, :]
        if kind == 0:
            w = sb_w_qkv[j].astype(BF16)
            wo = sb_w_o[j].astype(BF16)
            q, _, _, kt, vt = _sb_proj(xp, g_mix, w, sb_q_gain[j], sb_k_gain[j], nb_p)
            outs["sbk_p"].append(jnp.transpose(kt.reshape(nb_p, SB_KV_HEADS, SB_HEAD_DIM, t), (0, 3, 1, 2)))
            outs["sbv_p"].append(jnp.transpose(vt.reshape(nb_p, SB_KV_HEADS, SB_HEAD_DIM, t), (0, 3, 1, 2)))
            o = _sb_attn(q.reshape(nb_p, t, SB_NQ), kt, vt)
            xp = _proj_res(xp, o.reshape(nb_p * t, SB_NQ), wo)

            q, k, v, _, _ = _sb_proj(xs, g_mix, w, sb_q_gain[j], sb_k_gain[j], 1)
            outs["sbk_s"].append(k.reshape(nb_s, 1, SB_KV_HEADS, SB_HEAD_DIM))
            outs["sbv_s"].append(v.reshape(nb_s, 1, SB_KV_HEADS, SB_HEAD_DIM))
            qbd = (q.reshape(nb_s, SB_HEADS, 1, SB_HEAD_DIM) * kv_of_head[None, :, :, None].astype(BF16))
            qbd = qbd.reshape(nb_s, SB_HEADS, SB_NKV)
            ot = _sb_decode(qbd, k, v, sb_kc, sb_vc, j, page_table, sb_self_pos)
            ot = ot.reshape(nb_s, SB_KV_HEADS, SB_HEAD_DIM, SB_KV_HEADS, SB_GROUP)
            o = jnp.einsum('bkdkg->bkgd', ot).reshape(nb_s, SB_NQ)
            xs = _proj_res(xs, o.astype(BF16), wo)
        elif kind == 1:
            abr, abi, bbr, bbi = _s5_disc(s5_a_re[j], s5_a_im[j], s5_log_dt[j], s5_b_re[j], s5_b_im[j])
            wts = _s5_block_weights(bbr, bbi, s5_c_re[j], s5_c_im[j])
            abr, abi = abr.reshape(1, S5_DIM), abi.reshape(1, S5_DIM)
            dd = s5_d[j][None, :]
            wglu = s5_w_glu[j].astype(BF16)
            zeros = jnp.zeros((nb_p, 1, S5_DIM), F32)
            y, sr, si = _s5(xp.reshape(nb_p, t, d), g_mix, wts, abr, abi, zeros, zeros, dd, wglu, True)
            xp = y.reshape(nb_p * t, d)
            outs["s5r_p"].append(sr.reshape(nb_p, S5_GROUPS, S5_STATE))
            outs["s5i_p"].append(si.reshape(nb_p, S5_GROUPS, S5_STATE))
            h0r = state_s5_re[j].reshape(1, nb_s, S5_DIM)
            h0i = state_s5_im[j].reshape(1, nb_s, S5_DIM)
            y, sr, si = _s5(xs.reshape(1, nb_s, d), g_mix, wts, abr, abi, h0r, h0i, dd, wglu, False)
            xs = y.reshape(nb_s, d)
            outs["s5r_s"].append(sr.reshape(nb_s, S5_GROUPS, S5_STATE))
            outs["s5i_s"].append(si.reshape(nb_s, S5_GROUPS, S5_STATE))
        else:
            mw = _mla_weights(mla_w_dq[j], mla_q_ln[j], mla_w_uq[j], mla_w_dkv[j], mla_kv_ln[j],
                              mla_w_uk[j], mla_w_uv[j], mla_q_gain[j], mla_k_gain[j])
            wo = mla_w_o[j].astype(BF16)
            pe = slice(MLA_NOPE, MLA_QK)
            q, lat, kpet, kinvt, kt, v = _mla_proj(xp, g_mix, mw, pos_p, nb_p)
            outs["lat_p"].append(lat.reshape(nb_p, t, MLA_KV_RANK))
            outs["kpe_p"].append(jnp.transpose(kpet[:, pe, :], (0, 2, 1)))
            outs["kinv_p"].append(jnp.transpose(kinvt, (0, 2, 1)))
            o = _mla_attn(q.reshape(nb_p, t, -1), kt, v.reshape(nb_p, t, -1))
            xp = _proj_res(xp, o.reshape(nb_p * t, -1), wo)

            q, lat, kpet, kinvt, _, _ = _mla_proj(xs, g_mix, mw, jnp.broadcast_to(pos_s, (nb_s,)), 1)
            kpe_s = jnp.transpose(kpet[0, pe, :], (1, 0))
            kinv_s = jnp.transpose(kinvt[0], (1, 0))
            outs["lat_s"].append(lat.reshape(nb_s, 1, MLA_KV_RANK))
            outs["kpe_s"].append(kpe_s.reshape(nb_s, 1, MLA_ROPE))
            outs["kinv_s"].append(kinv_s.reshape(nb_s, 1, MLA_HEADS))
            qa = jnp.transpose(_mla_qabs(q, mw["wukt"]), (1, 0, 2))
            qp = q.reshape(nb_s, MLA_HEADS, MLA_SEG)[:, :, pe]
            ctx = _mla_decode(qa, qp, lat[:, None, :], kpe_s[:, None, :], kinv_s[:, :, None],
                              cache_mla_latent, mla_pc, mla_ic, j, page_table)
            o = _mla_ctx_out(jnp.transpose(ctx, (1, 0, 2)), mw["wuvh"])
            xs = _proj_res(xs, o, wo)
        g_mlp = ln_mlp[i][None, :]
        wu = w_up[i].astype(BF16)
        wd = w_down[i].astype(BF16)
        xp = _mlp(xp, g_mlp, wu, wd)
        xs = _mlp(xs, g_mlp, wu, wd)

    st = lambda k: jnp.stack(outs[k])
    return (xp.reshape(nb_p, t, d), xs.reshape(nb_s, t_s, d),
            st("sbk_p"), st("sbv_p"), st("sbk_s"), st("sbv_s"),
            st("lat_p"), st("kpe_p"), st("kinv_p"), st("lat_s"), st("kpe_s"), st("kinv_s"),
            st("s5r_p"), st("s5i_p"), st("s5r_s"), st("s5i_s"))
```

```python
import functools
import math

import numpy as np
import jax
import jax.numpy as jnp
from jax import lax
from jax.experimental import pallas as pl
from jax.experimental.pallas import tpu as pltpu

F32 = jnp.float32
BF16 = jnp.bfloat16

D_MODEL = 1024
DEPTH = 4
PAGE_SIZE = 128
N_MIXERS = 3
D_FF = 4 * D_MODEL
SB_HEADS = 16
SB_KV_HEADS = 4
SB_HEAD_DIM = 64
SB_GROUP = SB_HEADS // SB_KV_HEADS
SB_SCALE = SB_HEAD_DIM ** -0.5
SB_NQ = SB_HEADS * SB_HEAD_DIM
SB_NKV = SB_KV_HEADS * SB_HEAD_DIM
S5_GROUP_CH = 16
S5_GROUPS = D_MODEL // S5_GROUP_CH
S5_STATE = 64
S5_DIM = S5_GROUPS * S5_STATE
MLA_HEADS = 16
MLA_Q_RANK = 384
MLA_KV_RANK = 256
MLA_NOPE = 64
MLA_ROPE = 32
MLA_V = 64
MLA_QK = MLA_NOPE + MLA_ROPE
MLA_SCALE = MLA_QK ** -0.5
ROPE_THETA = 10000.0
RMS_EPS = 1e-6
NEG_INF = -1e30

LANES = 128
MLA_SEG = LANES
VMEM_LIMIT_BYTES = 56 * 1024 * 1024


def _cparams(*sem):
    return pltpu.CompilerParams(dimension_semantics=sem, vmem_limit_bytes=VMEM_LIMIT_BYTES)


def _rms(x, g):
    xf = x.astype(F32)
    return xf * lax.rsqrt(jnp.mean(xf * xf, axis=-1, keepdims=True) + RMS_EPS) * g


def _dot(a, b):
    return jnp.dot(a, b, preferred_element_type=F32)


def _dot_nt(a, b):
    return lax.dot_general(a, b, (((1,), (1,)), ((), ())), preferred_element_type=F32)


def _split(x):
    hi = x.astype(BF16)
    lo = (x - hi.astype(F32)).astype(BF16)
    return hi, lo


def _sdot(x, s):
    hi, lo = _split(x)
    return _dot(hi, s) + _dot(lo, s)


def _sdot_nt(s, x):
    hi, lo = _split(x)
    return _dot_nt(s, hi) + _dot_nt(s, lo)


def _row_tile(m, want):
    t = min(m, want)
    assert m % t == 0
    return t


def _mlp_kernel(x_ref, g_ref, wu_ref, wd_ref, o_ref, h_ref, acc_ref):
    f = pl.program_id(1)

    @pl.when(f == 0)
    def _():
        h_ref[...] = _rms(x_ref[...], g_ref[...]).astype(BF16)
        acc_ref[...] = jnp.zeros_like(acc_ref)

    a = jnp.maximum(_dot(h_ref[...], wu_ref[...]), 0.0)
    acc_ref[...] += _dot((a * a).astype(BF16), wd_ref[...])

    @pl.when(f == pl.num_programs(1) - 1)
    def _():
        o_ref[...] = x_ref[...] + acc_ref[...]


def _mlp(x, g, wu, wd):
    m = x.shape[0]
    tm = _row_tile(m, 1024)
    tf = 512
    return pl.pallas_call(
        _mlp_kernel,
        grid=(m // tm, D_FF // tf),
        in_specs=[
            pl.BlockSpec((tm, D_MODEL), lambda i, f: (i, 0)),
            pl.BlockSpec((1, D_MODEL), lambda i, f: (0, 0)),
            pl.BlockSpec((D_MODEL, tf), lambda i, f: (0, f)),
            pl.BlockSpec((tf, D_MODEL), lambda i, f: (f, 0)),
        ],
        out_specs=pl.BlockSpec((tm, D_MODEL), lambda i, f: (i, 0)),
        out_shape=jax.ShapeDtypeStruct((m, D_MODEL), F32),
        scratch_shapes=[pltpu.VMEM((tm, D_MODEL), BF16), pltpu.VMEM((tm, D_MODEL), F32)],
        compiler_params=_cparams("parallel", "arbitrary"),
        name="mlp",
    )(x, g, wu, wd)


def _proj_res_kernel(x_ref, o_ref, w_ref, y_ref):
    y_ref[...] = x_ref[...] + _dot(o_ref[...], w_ref[...])


def _proj_res(x, o, w):
    m, k = o.shape
    tm = _row_tile(m, 1024)
    return pl.pallas_call(
        _proj_res_kernel,
        grid=(m // tm,),
        in_specs=[
            pl.BlockSpec((tm, D_MODEL), lambda i: (i, 0)),
            pl.BlockSpec((tm, k), lambda i: (i, 0)),
            pl.BlockSpec((k, D_MODEL), lambda i: (0, 0)),
        ],
        out_specs=pl.BlockSpec((tm, D_MODEL), lambda i: (i, 0)),
        out_shape=jax.ShapeDtypeStruct((m, D_MODEL), F32),
        compiler_params=_cparams("parallel"),
        name="proj_res",
    )(x, o, w)


def _seg_matrix(width, seg, cols=LANES):
    r = np.arange(width)[:, None] // seg
    c = np.arange(cols)[None, :]
    return jnp.asarray((r == c).astype(np.float32), dtype=BF16)


def _sb_proj_kernel(x_ref, g_ref, w_ref, seg_ref, segt_ref, gain_ref, q_ref, k_ref, v_ref, kt_ref, vt_ref):
    h = _rms(x_ref[...], g_ref[...]).astype(BF16)
    qkv = _dot(h, w_ref[...])
    nqk = SB_NQ + SB_NKV
    qk = qkv[:, :nqk]
    ms = _sdot(qk * qk, seg_ref[...]) * (1.0 / SB_HEAD_DIM)
    scale = _sdot(lax.rsqrt(ms + RMS_EPS), segt_ref[...])
    qkn = qk * scale * gain_ref[...]
    kn = qkn[:, SB_NQ:]
    v = qkv[:, nqk:]
    q_ref[...] = qkn[:, :SB_NQ].astype(BF16)
    k_ref[...] = kn
    v_ref[...] = v
    kt_ref[0] = kn.T
    vt_ref[0] = v.T


def _sb_proj(x, g, w, gq, gk, nseq):
    m = x.shape[0]
    t = m // nseq
    tm = _row_tile(t, 512)
    nt = t // tm
    nqk = SB_NQ + SB_NKV
    seg = _seg_matrix(nqk, SB_HEAD_DIM)
    gain = jnp.concatenate([jnp.tile(gq * SB_SCALE, SB_HEADS), jnp.tile(gk, SB_KV_HEADS)])[None, :]
    return pl.pallas_call(
        _sb_proj_kernel,
        grid=(m // tm,),
        in_specs=[
            pl.BlockSpec((tm, D_MODEL), lambda i: (i, 0)),
            pl.BlockSpec((1, D_MODEL), lambda i: (0, 0)),
            pl.BlockSpec(w.shape, lambda i: (0, 0)),
            pl.BlockSpec(seg.shape, lambda i: (0, 0)),
            pl.BlockSpec(seg.T.shape, lambda i: (0, 0)),
            pl.BlockSpec(gain.shape, lambda i: (0, 0)),
        ],
        out_specs=[
            pl.BlockSpec((tm, SB_NQ), lambda i: (i, 0)),
            pl.BlockSpec((tm, SB_NKV), lambda i: (i, 0)),
            pl.BlockSpec((tm, SB_NKV), lambda i: (i, 0)),
            pl.BlockSpec((1, SB_NKV, tm), lambda i: (i // nt, 0, i % nt)),
            pl.BlockSpec((1, SB_NKV, tm), lambda i: (i // nt, 0, i % nt)),
        ],
        out_shape=[
            jax.ShapeDtypeStruct((m, SB_NQ), BF16),
            jax.ShapeDtypeStruct((m, SB_NKV), F32),
            jax.ShapeDtypeStruct((m, SB_NKV), F32),
            jax.ShapeDtypeStruct((nseq, SB_NKV, t), F32),
            jax.ShapeDtypeStruct((nseq, SB_NKV, t), F32),
        ],
        compiler_params=_cparams("parallel"),
        name="sb_proj",
    )(x, g, w, seg, seg.T, gain)


def _tri_ones():
    j = np.arange(LANES)[:, None]
    s = np.arange(LANES)[None, :]
    u = np.concatenate([(j > s).astype(np.float32), np.ones((LANES, LANES), np.float32)], axis=1)
    return jnp.asarray(np.concatenate([u, u], axis=0), dtype=BF16)


def _log_sigmoid_neg(z):
    return jnp.minimum(-z, 0.0) - jnp.log(1.0 + jnp.exp(-jnp.abs(z)))


SB_CUM = 2 * LANES
LOG2E = math.log2(math.e)


def _tri_upper():
    j = np.arange(SB_CUM)[:, None]
    s = np.arange(SB_CUM)[None, :]
    return jnp.asarray((j > s).astype(np.float32), dtype=BF16)


SB_DEAD = 110.0


def _sb_attn_kernel(qi_ref, kv_ref, q_ref, kt_ref, vt_ref, u_ref, o_ref, acc_ref, c_ref, dead_ref, *, tq, tk):
    s = pl.program_id(2)
    i = qi_ref[s]
    kv = kv_ref[s]
    r = tq // tk
    first = kv == (i + 1) * r - 1
    last = kv == 0

    @pl.when(first)
    def _():
        acc_ref[...] = jnp.zeros_like(acc_ref)
        c_ref[...] = jnp.zeros_like(c_ref)
        dead_ref[0] = 0

    def step(masked):
        kt = kt_ref[0].astype(BF16)
        vt = vt_ref[0].astype(BF16)
        u = u_ref[...]
        if masked:
            row = lax.broadcasted_iota(jnp.int32, (tq, tk), 0) + i * tq
            col = lax.broadcasted_iota(jnp.int32, (tq, tk), 1) + kv * tk
            valid = col < row
        for g in range(SB_GROUP):
            qg = q_ref[0, :, g * SB_HEAD_DIM:(g + 1) * SB_HEAD_DIM]
            z = _dot(qg, kt)
            sp = jnp.maximum(z, 0.0) + jnp.log(1.0 + jnp.exp2(jnp.abs(z) * (-LOG2E)))
            if masked:
                sp = jnp.where(valid, sp, 0.0)
            spb = sp.astype(BF16)
            c = c_ref[g]
            nsub = tk // SB_CUM
            tails = [None] * nsub
            for sb in reversed(range(nsub)):
                blk = spb[:, sb * SB_CUM:(sb + 1) * SB_CUM]
                later = _dot(blk, u)
                tails[sb] = later + jnp.concatenate([c, c], axis=1)
                c = c + (later[:, :1] + blk[:, :1].astype(F32))
            c_ref[g] = c
            w = jnp.exp(z - sp - jnp.concatenate(tails, axis=1))
            if masked:
                w = jnp.where(valid, w, 0.0)
            acc_ref[g] += _dot_nt(w.astype(BF16), vt)
        cmin = functools.reduce(jnp.minimum, [c_ref[g] for g in range(SB_GROUP)])
        dead_ref[0] = (jnp.min(cmin) >= SB_DEAD).astype(jnp.int32)

    diag = kv >= i * r
    live = dead_ref[0] == 0

    @pl.when(jnp.logical_and(diag, live))
    def _():
        step(True)

    @pl.when(jnp.logical_and(jnp.logical_not(diag), live))
    def _():
        step(False)

    @pl.when(last)
    def _():
        o_ref[0] = jnp.concatenate([acc_ref[g] for g in range(SB_GROUP)], axis=1).astype(BF16)


def _causal_pairs(nq, r, reverse):
    qi, kv = [], []
    for i in range(nq):
        ks = list(range((i + 1) * r))
        if reverse:
            ks = ks[::-1]
        qi += [i] * len(ks)
        kv += ks
    return jnp.asarray(qi, jnp.int32), jnp.asarray(kv, jnp.int32)


def _sb_attn(q, kt, vt):
    b, t, _ = q.shape
    tq = _row_tile(t, 512)
    tk = tq
    qi, kv = _causal_pairs(t // tq, tq // tk, reverse=True)
    gw = SB_GROUP * SB_HEAD_DIM
    uo = _tri_upper()
    grid_spec = pltpu.PrefetchScalarGridSpec(
        num_scalar_prefetch=2,
        grid=(b, SB_KV_HEADS, int(qi.shape[0])),
        in_specs=[
            pl.BlockSpec((1, tq, gw), lambda bb, h, s, qi_r, kv_r: (bb, qi_r[s], h)),
            pl.BlockSpec((1, SB_HEAD_DIM, tk), lambda bb, h, s, qi_r, kv_r: (bb, h, kv_r[s])),
            pl.BlockSpec((1, SB_HEAD_DIM, tk), lambda bb, h, s, qi_r, kv_r: (bb, h, kv_r[s])),
            pl.BlockSpec(uo.shape, lambda bb, h, s, qi_r, kv_r: (0, 0)),
        ],
        out_specs=pl.BlockSpec((1, tq, gw), lambda bb, h, s, qi_r, kv_r: (bb, qi_r[s], h)),
        scratch_shapes=[
            pltpu.VMEM((SB_GROUP, tq, SB_HEAD_DIM), F32),
            pltpu.VMEM((SB_GROUP, tq, LANES), F32),
            pltpu.SMEM((1,), jnp.int32),
        ],
    )
    return pl.pallas_call(
        functools.partial(_sb_attn_kernel, tq=tq, tk=tk),
        grid_spec=grid_spec,
        out_shape=jax.ShapeDtypeStruct((b, t, SB_NQ), BF16),
        compiler_params=_cparams("parallel", "parallel", "arbitrary"),
        name="sb_attn",
    )(qi, kv, q, kt, vt, uo)


SB_DEC_WINDOW = 4


def _page_copies(pt_ref, seq, pages, dsts, sems, *, npages, layer, pools):
    out = []
    for pool, dst, sem in zip(pools, dsts, sems):
        for k, p in enumerate(pages):
            pg = pt_ref[seq * npages + p]
            out.append(pltpu.make_async_copy(
                pool.at[layer, pg], dst.at[:, pl.ds(k * PAGE_SIZE, PAGE_SIZE)], sem))
    return out


def _sb_decode_kernel(pt_ref, vs_ref, qbd_ref, qbdt_ref, knr_ref, knc_ref, vnc_ref, uo_ref, kc_ref, vc_ref,
                      o_ref, kwin, vwin, kold, vold, sem, sem_old, *, npages, layer, nwin):
    b = pl.program_id(0)
    nb = pl.num_programs(0)
    slot = b % 2
    nold = npages - nwin
    copies = functools.partial(_page_copies, pt_ref, npages=npages, layer=layer, pools=(kc_ref, vc_ref))
    win = lambda seq, sl: copies(seq, range(nold, npages), (kwin.at[sl], vwin.at[sl]), (sem.at[0, sl], sem.at[1, sl]))
    old = lambda seq: copies(seq, range(nold), (kold, vold), (sem_old.at[0], sem_old.at[1]))

    @pl.when(b == 0)
    def _():
        for cp in win(0, 0):
            cp.start()

    @pl.when(b + 1 < nb)
    def _():
        for cp in win(b + 1, 1 - slot):
            cp.start()

    for cp in win(b, slot):
        cp.wait()

    qbd = qbd_ref[0]
    nh = qbd.shape[0]
    self_valid = vs_ref[0] < vs_ref[1]
    z_col = jnp.sum(qbd.astype(F32) * knr_ref[0], axis=1, keepdims=True)
    l_col = jnp.where(self_valid, _log_sigmoid_neg(z_col), 0.0)
    z_row = jnp.sum(qbdt_ref[0].astype(F32) * knc_ref[0], axis=0, keepdims=True)
    w_row = jnp.where(self_valid, jnp.exp(z_row + _log_sigmoid_neg(z_row)), 0.0)

    def attend(kt, vt, n, c):
        z_all = _dot(qbd, kt)
        z = jnp.concatenate([z_all[:, p * PAGE_SIZE:(p + 1) * PAGE_SIZE] for p in range(n)], axis=0)
        lneg = _log_sigmoid_neg(z)
        hi, lo = _split(lneg)
        res = _dot(jnp.concatenate([hi, lo], axis=1), uo_ref[...])
        tails = [None] * n
        for p in reversed(range(n)):
            rs = slice(p * nh, (p + 1) * nh)
            tails[p] = res[rs, :PAGE_SIZE] + c
            c = c + res[rs, PAGE_SIZE:]
        w = jnp.exp(z + lneg + jnp.concatenate(tails, axis=0))
        w_all = jnp.concatenate([w[p * nh:(p + 1) * nh] for p in range(n)], axis=1)
        return _dot_nt(vt, w_all), c

    ot, c = attend(kwin[slot], vwin[slot], nwin, jnp.broadcast_to(l_col, (nh, PAGE_SIZE)))
    o_ref[0] = ot + vnc_ref[0] * w_row

    if nold:
        @pl.when(jnp.max(c) > -SB_DEAD)
        def _():
            for cp in old(b):
                cp.start()
            for cp in old(b):
                cp.wait()
            o_ref[0] += attend(kold[...], vold[...], nold, c)[0]


def _sb_decode(qbd, k_new, v_new, kc, vc, layer, page_table, self_pos):
    nb, nh, kw = qbd.shape
    npages = page_table.shape[1]
    nwin = min(SB_DEC_WINDOW, npages)
    nold = npages - nwin
    uo = _tri_ones()
    im = lambda b, pt, vs: (b, 0, 0)
    grid_spec = pltpu.PrefetchScalarGridSpec(
        num_scalar_prefetch=2,
        grid=(nb,),
        in_specs=[
            pl.BlockSpec((1, nh, kw), im),
            pl.BlockSpec((1, kw, nh), im),
            pl.BlockSpec((1, 1, kw), im),
            pl.BlockSpec((1, kw, 1), im),
            pl.BlockSpec((1, kw, 1), im),
            pl.BlockSpec(uo.shape, lambda b, pt, vs: (0, 0)),
            pl.BlockSpec(memory_space=pl.ANY),
            pl.BlockSpec(memory_space=pl.ANY),
        ],
        out_specs=pl.BlockSpec((1, kw, nh), im),
        scratch_shapes=[
            pltpu.VMEM((2, kw, nwin * PAGE_SIZE), F32),
            pltpu.VMEM((2, kw, nwin * PAGE_SIZE), F32),
            pltpu.VMEM((kw, max(nold, 1) * PAGE_SIZE), F32),
            pltpu.VMEM((kw, max(nold, 1) * PAGE_SIZE), F32),
            pltpu.SemaphoreType.DMA((2, 2)),
            pltpu.SemaphoreType.DMA((2,)),
        ],
    )
    return pl.pallas_call(
        functools.partial(_sb_decode_kernel, npages=npages, layer=layer, nwin=nwin),
        grid_spec=grid_spec,
        out_shape=jax.ShapeDtypeStruct((nb, kw, nh), F32),
        compiler_params=_cparams("arbitrary"),
        name="sb_decode",
    )(page_table.reshape(-1), self_pos, qbd, jnp.transpose(qbd, (0, 2, 1)), k_new[:, None, :],
      k_new[:, :, None], v_new[:, :, None], uo, kc, vc)


def _s5_disc_kernel(are_ref, aim_ref, ldt_ref, arex_ref, aimx_ref, ldtx_ref, bre_ref, bim_ref,
                    abr_ref, abi_ref, bbr_ref, bbi_ref):
    def zoh(a_re, a_im, log_dt):
        dt = jnp.exp(log_dt)
        mag = jnp.exp(dt * a_re)
        ab_re = mag * jnp.cos(dt * a_im)
        ab_im = mag * jnp.sin(dt * a_im)
        den = a_re * a_re + a_im * a_im
        xr, yi = ab_re - 1.0, ab_im
        return ab_re, ab_im, (xr * a_re + yi * a_im) / den, (yi * a_re - xr * a_im) / den

    ab_re, ab_im, _, _ = zoh(are_ref[...], aim_ref[...], ldt_ref[...])
    abr_ref[...] = ab_re
    abi_ref[...] = ab_im
    _, _, c_re, c_im = zoh(arex_ref[...], aimx_ref[...], ldtx_ref[...])
    b_re = bre_ref[...]
    b_im = bim_ref[...]
    bbr_ref[...] = c_re * b_re - c_im * b_im
    bbi_ref[...] = c_re * b_im + c_im * b_re


def _s5_disc(a_re, a_im, log_dt, b_re, b_im):
    g, p = a_re.shape
    c = b_re.shape[-1]
    rep = lambda a: jnp.repeat(a, c, axis=1)
    ldt = jnp.broadcast_to(log_dt[:, None], (g, p))
    args = (a_re, a_im, ldt, rep(a_re), rep(a_im), rep(ldt), b_re.reshape(g, p * c), b_im.reshape(g, p * c))
    small = jax.ShapeDtypeStruct((g, p), F32)
    wide = jax.ShapeDtypeStruct((g, p * c), F32)
    return pl.pallas_call(
        _s5_disc_kernel,
        out_shape=[small, small, wide, wide],
        name="s5_disc",
    )(*args)


def _s5_block_weights(bb_re, bb_im, c_re, c_im):
    g, c, p = S5_GROUPS, S5_GROUP_CH, S5_STATE
    nblk = 4
    gl = g // nblk
    eye = jnp.eye(gl, dtype=F32)

    def in_blk(bb):
        bb = bb.reshape(nblk, gl, p, c)
        return jnp.einsum('kgpc,gh->kgchp', bb, eye).reshape(nblk, gl * c, gl * p).astype(BF16)

    def out_blk(cc):
        cc = cc.reshape(nblk, gl, c, p)
        return jnp.einsum('kgcp,gh->kgphc', cc, eye).reshape(nblk, gl * p, gl * c).astype(BF16)

    return in_blk(bb_re), in_blk(bb_im), out_blk(c_re), out_blk(c_im)


def _s5_kernel(x_ref, g_ref, bre_ref, bim_ref, abr_ref, abi_ref, h0r_ref, h0i_ref, cre_ref, cim_ref,
               d_ref, wg_ref, o_ref, sr_ref, si_ref, str_ref, sti_ref, hr_ref, hi_ref, *, sequential):
    nblk = bre_ref.shape[0]
    cw = D_MODEL // nblk
    sw = S5_DIM // nblk
    x = x_ref[0]
    tt = x.shape[0]
    u = _rms(x, g_ref[...])
    ub = u.astype(BF16)
    for k in range(nblk):
        uk = ub[:, k * cw:(k + 1) * cw]
        str_ref[:, k * sw:(k + 1) * sw] = _dot(uk, bre_ref[k])
        sti_ref[:, k * sw:(k + 1) * sw] = _dot(uk, bim_ref[k])

    if sequential:
        @pl.when(pl.program_id(1) == 0)
        def _():
            hr_ref[...] = h0r_ref[0]
            hi_ref[...] = h0i_ref[0]

        for k in range(nblk):
            cs = slice(k * sw, (k + 1) * sw)
            ar = abr_ref[:, cs]
            ai = abi_ref[:, cs]

            def body(r, carry):
                hr, hi = carry
                br = str_ref[pl.ds(r, 1), cs]
                bi = sti_ref[pl.ds(r, 1), cs]
                nr = ar * hr - ai * hi + br
                ni = ar * hi + ai * hr + bi
                str_ref[pl.ds(r, 1), cs] = nr
                sti_ref[pl.ds(r, 1), cs] = ni
                return nr, ni

            hr, hi = lax.fori_loop(0, tt, body, (hr_ref[:, cs], hi_ref[:, cs]))
            hr_ref[:, cs] = hr
            hi_ref[:, cs] = hi
        sr_ref[0] = hr_ref[...]
        si_ref[0] = hi_ref[...]
    else:
        ar = abr_ref[...]
        ai = abi_ref[...]
        h0r = h0r_ref[0]
        h0i = h0i_ref[0]
        nr = ar * h0r - ai * h0i + str_ref[...]
        ni = ar * h0i + ai * h0r + sti_ref[...]
        str_ref[...] = nr
        sti_ref[...] = ni
        sr_ref[0] = nr
        si_ref[0] = ni

    ys = []
    for k in range(nblk):
        cs = slice(k * sw, (k + 1) * sw)
        ys.append(_dot(str_ref[:, cs].astype(BF16), cre_ref[k]) - _dot(sti_ref[:, cs].astype(BF16), cim_ref[k]))
    y = jnp.concatenate(ys, axis=1) + d_ref[...] * u
    y = jax.nn.gelu(y)
    vg = _dot(y.astype(BF16), wg_ref[...])
    o_ref[0] = x + vg[:, :D_MODEL] * jax.nn.sigmoid(vg[:, D_MODEL:])


def _s5(x, g, wts, abr, abi, h0r, h0i, d, wglu, sequential):
    b, t, _ = x.shape
    tt = _row_tile(t, 256)
    bre, bim, cre, cim = wts
    srow = 1 if sequential else tt
    st_spec = pl.BlockSpec((1, srow, S5_DIM), (lambda bb, i: (bb, 0, 0)) if sequential else (lambda bb, i: (bb, i, 0)))
    full = lambda a: pl.BlockSpec(a.shape, lambda bb, i: (0,) * a.ndim)
    return pl.pallas_call(
        functools.partial(_s5_kernel, sequential=sequential),
        grid=(b, t // tt),
        in_specs=[
            pl.BlockSpec((1, tt, D_MODEL), lambda bb, i: (bb, i, 0)),
            full(g), full(bre), full(bim), full(abr), full(abi),
            st_spec, st_spec,
            full(cre), full(cim), full(d), full(wglu),
        ],
        out_specs=[pl.BlockSpec((1, tt, D_MODEL), lambda bb, i: (bb, i, 0)), st_spec, st_spec],
        out_shape=[
            jax.ShapeDtypeStruct((b, t, D_MODEL), F32),
            jax.ShapeDtypeStruct((b, t if not sequential else 1, S5_DIM), F32),
            jax.ShapeDtypeStruct((b, t if not sequential else 1, S5_DIM), F32),
        ],
        scratch_shapes=[
            pltpu.VMEM((tt, S5_DIM), F32), pltpu.VMEM((tt, S5_DIM), F32),
            pltpu.VMEM((1, S5_DIM), F32), pltpu.VMEM((1, S5_DIM), F32),
        ],
        compiler_params=_cparams("parallel", "arbitrary"),
        name="s5_seq" if sequential else "s5_step",
    )(x, g, bre, bim, abr, abi, h0r, h0i, cre, cim, d, wglu)


def _seg_cols(w_nope, w_pe):
    k, h, _ = w_nope.shape
    w_pe = jnp.broadcast_to(w_pe, (k, h, MLA_ROPE))
    pad = jnp.zeros((k, h, MLA_SEG - MLA_QK), w_nope.dtype)
    return jnp.concatenate([w_nope, w_pe, pad], axis=-1).reshape(k, h * MLA_SEG)


def _seg_vec(v_nope, v_pe, fill=0.0):
    pad = jnp.full((MLA_SEG - MLA_QK,), fill, F32)
    return jnp.concatenate([v_nope, v_pe, pad])[None, :]


def _rope_tables(pos):
    half = MLA_ROPE // 2
    freqs = ROPE_THETA ** (-jnp.arange(half, dtype=F32) / half)
    ang = pos.astype(F32)[:, None] * freqs[None, :]
    cos, sin = jnp.cos(ang), jnp.sin(ang)
    n = pos.shape[0]
    one = jnp.ones((n, MLA_NOPE), F32)
    zero = jnp.zeros((n, MLA_NOPE), F32)
    z32 = jnp.zeros((n, MLA_SEG - MLA_QK), F32)
    z16 = jnp.zeros((n, half), F32)
    c = jnp.concatenate([one, cos, cos, z32], axis=1)
    s_lo = jnp.concatenate([zero, -sin, z16, z32], axis=1)
    s_hi = jnp.concatenate([zero, z16, sin, z32], axis=1)
    return c, s_lo, s_hi


def _rope_seg(x, c, s_lo, s_hi):
    half = MLA_ROPE // 2
    up = pltpu.roll(x, MLA_SEG - half, 1)
    dn = pltpu.roll(x, half, 1)
    return x * c + up * s_lo + dn * s_hi


def _mla_proj_kernel(x_ref, g_ref, wdq_ref, gql_ref, wuq_ref, gq_ref, gkq_ref, wdkv_ref, gkvl_ref,
                     wuk_ref, wuv_ref, vone_ref, gkpe_ref, segt_ref, c_ref, slo_ref, shi_ref,
                     q_ref, lat_ref, kpe_ref, kinvt_ref, kt_ref, v_ref):
    h = _rms(x_ref[...], g_ref[...]).astype(BF16)
    c, s_lo, s_hi = c_ref[...], slo_ref[...], shi_ref[...]
    cq = _rms(_dot(h, wdq_ref[...]), gql_ref[...]).astype(BF16)
    qraw = _dot(cq, wuq_ref[...])
    for hh in range(MLA_HEADS):
        qs = qraw[:, hh * MLA_SEG:(hh + 1) * MLA_SEG]
        qs = qs * lax.rsqrt(jnp.sum(qs * qs, axis=1, keepdims=True) * (1.0 / MLA_QK) + RMS_EPS) * gq_ref[...]
        qs = _rope_seg(qs * gkq_ref[...], c, s_lo, s_hi)
        q_ref[:, hh * MLA_SEG:(hh + 1) * MLA_SEG] = qs.astype(BF16)
    ckv = _dot(h, wdkv_ref[...])
    lat = _rms(ckv[:, :MLA_KV_RANK], gkvl_ref[...])
    lat_ref[...] = lat
    latb = lat.astype(BF16)
    kpe_raw = ckv[:, MLA_KV_RANK:]
    kn = _dot(latb, wuk_ref[...])
    ones = jnp.ones((8, MLA_SEG), BF16)
    pe_ss = _sdot_nt(ones, kpe_raw * kpe_raw)[:1]
    ms = (_sdot_nt(segt_ref[...], kn * kn) + pe_ss) * (1.0 / MLA_QK)
    kinvt = lax.rsqrt(ms + RMS_EPS)
    kinvt_ref[0] = kinvt
    kpe = _rope_seg(kpe_raw * gkpe_ref[...], c, s_lo, s_hi)
    kpe_ref[0] = kpe.T
    for hh in range(MLA_HEADS):
        ks = (kn[:, hh * MLA_SEG:(hh + 1) * MLA_SEG] + kpe).T
        kt_ref[0, hh] = (ks * (kinvt[hh:hh + 1] * (MLA_SCALE * LOG2E))).astype(BF16)
    v_ref[...] = (_dot(latb, wuv_ref[...]) + vone_ref[...]).astype(BF16)


def _mla_weights(w_dq, g_ql, w_uq, w_dkv, g_kvl, w_uk, w_uv, g_q, g_k):
    half = MLA_ROPE // 2
    uq = w_uq.reshape(MLA_Q_RANK, MLA_HEADS, MLA_QK)
    wuq = _seg_cols(uq[..., :MLA_NOPE], uq[..., MLA_NOPE:]).astype(BF16)
    kvn = jnp.zeros((D_MODEL, 1, MLA_NOPE), F32)
    wdkv = jnp.concatenate([w_dkv[:, :MLA_KV_RANK], _seg_cols(kvn, w_dkv[:, None, MLA_KV_RANK:])], axis=1).astype(BF16)
    wuk = _seg_cols(w_uk, jnp.zeros((MLA_KV_RANK, 1, MLA_ROPE), F32)).astype(BF16)
    return dict(
        wdq=w_dq.astype(BF16), gql=g_ql[None, :], wuq=wuq,
        gq=_seg_vec(g_q[:MLA_NOPE], g_q[MLA_NOPE:]),
        gkq=_seg_vec(g_k[:MLA_NOPE], jnp.ones((MLA_ROPE,), F32)),
        wdkv=wdkv, gkvl=g_kvl[None, :], wuk=wuk,
        wuv=jnp.concatenate([w_uv, jnp.zeros_like(w_uv)], axis=-1).reshape(MLA_KV_RANK, -1).astype(BF16),
        vone=jnp.tile(jnp.concatenate([jnp.zeros((MLA_V,), F32), jnp.ones((MLA_SEG - MLA_V,), F32)]),
                      MLA_HEADS)[None, :],
        gkpe=_seg_vec(jnp.zeros((MLA_NOPE,), F32), g_k[MLA_NOPE:]),
        segt=_seg_matrix(MLA_HEADS * MLA_SEG, MLA_SEG, cols=MLA_HEADS).T,
        wukt=jnp.transpose(w_uk, (1, 2, 0)).astype(BF16),
        wuvh=jnp.transpose(w_uv, (1, 0, 2)).astype(BF16),
    )


def _mla_proj(x, g, mw, pos, nseq):
    m = x.shape[0]
    t = m // nseq
    tm = _row_tile(t, 256)
    nt = t // tm
    tabs = _rope_tables(pos)
    hs = MLA_HEADS * MLA_SEG
    full = lambda a: pl.BlockSpec(a.shape, lambda i: (0,) * a.ndim)
    tab_spec = pl.BlockSpec((tm, MLA_SEG), lambda i: (i % nt, 0))
    tmaj = lambda rows: pl.BlockSpec((1, rows, tm), lambda i: (i // nt, 0, i % nt))
    ws = [mw[k] for k in ("wdq", "gql", "wuq", "gq", "gkq", "wdkv", "gkvl", "wuk", "wuv", "vone", "gkpe", "segt")]
    return pl.pallas_call(
        _mla_proj_kernel,
        grid=(m // tm,),
        in_specs=[pl.BlockSpec((tm, D_MODEL), lambda i: (i, 0)), full(g)] + [full(w) for w in ws]
        + [tab_spec] * 3,
        out_specs=[
            pl.BlockSpec((tm, hs), lambda i: (i, 0)),
            pl.BlockSpec((tm, MLA_KV_RANK), lambda i: (i, 0)),
            tmaj(MLA_SEG),
            tmaj(MLA_HEADS),
            pl.BlockSpec((1, MLA_HEADS, MLA_SEG, tm), lambda i: (i // nt, 0, 0, i % nt)),
            pl.BlockSpec((tm, hs), lambda i: (i, 0)),
        ],
        out_shape=[
            jax.ShapeDtypeStruct((m, hs), BF16),
            jax.ShapeDtypeStruct((m, MLA_KV_RANK), F32),
            jax.ShapeDtypeStruct((nseq, MLA_SEG, t), F32),
            jax.ShapeDtypeStruct((nseq, MLA_HEADS, t), F32),
            jax.ShapeDtypeStruct((nseq, MLA_HEADS, MLA_SEG, t), BF16),
            jax.ShapeDtypeStruct((m, hs), BF16),
        ],
        compiler_params=_cparams("parallel"),
        name="mla_proj",
    )(x, g, *ws, *tabs)


MLA_HPS = 4


def _mla_attn_kernel(qi_ref, kv_ref, q_ref, kt_ref, v_ref, o_ref, m_ref, acc_ref, *, tq, tk):
    s = pl.program_id(2)
    i = qi_ref[s]
    kv = kv_ref[s]
    r = tq // tk

    @pl.when(kv == 0)
    def _():
        m_ref[...] = jnp.full_like(m_ref, NEG_INF)
        acc_ref[...] = jnp.zeros_like(acc_ref)

    def step(masked):
        if masked:
            row = lax.broadcasted_iota(jnp.int32, (tq, tk), 0) + i * tq
            col = lax.broadcasted_iota(jnp.int32, (tq, tk), 1) + kv * tk
            valid = col <= row
        for hh in range(MLA_HPS):
            sc = _dot(q_ref[0, :, hh * MLA_SEG:(hh + 1) * MLA_SEG], kt_ref[0, hh])
            if masked:
                sc = jnp.where(valid, sc, NEG_INF)
            m_prev = m_ref[hh]
            m_new = jnp.maximum(m_prev, jnp.max(sc, axis=1, keepdims=True))
            p = jnp.exp2(sc - m_new[:, :1])
            pv = _dot(p.astype(BF16), v_ref[0, :, hh * MLA_SEG:(hh + 1) * MLA_SEG])
            acc_ref[hh] = acc_ref[hh] * jnp.exp2(m_prev - m_new) + pv
            m_ref[hh] = m_new

    diag = kv >= i * r

    @pl.when(diag)
    def _():
        step(True)

    @pl.when(jnp.logical_not(diag))
    def _():
        step(False)

    @pl.when(kv == (i + 1) * r - 1)
    def _():
        outs = []
        for hh in range(MLA_HPS):
            a = acc_ref[hh]
            outs.append((a / pltpu.roll(a, MLA_V, 1))[:, :MLA_V])
        o_ref[0] = jnp.concatenate(outs, axis=1).astype(BF16)


def _mla_attn(q, kt, v):
    b, t, _ = q.shape
    tq = _row_tile(t, 512)
    tk = tq
    qi, kvt = _causal_pairs(t // tq, tq // tk, reverse=False)
    grid_spec = pltpu.PrefetchScalarGridSpec(
        num_scalar_prefetch=2,
        grid=(b, MLA_HEADS // MLA_HPS, int(qi.shape[0])),
        in_specs=[
            pl.BlockSpec((1, tq, MLA_HPS * MLA_SEG), lambda bb, h, s, qi_r, kv_r: (bb, qi_r[s], h)),
            pl.BlockSpec((1, MLA_HPS, MLA_SEG, tk), lambda bb, h, s, qi_r, kv_r: (bb, h, 0, kv_r[s])),
            pl.BlockSpec((1, tk, MLA_HPS * MLA_SEG), lambda bb, h, s, qi_r, kv_r: (bb, kv_r[s], h)),
        ],
        out_specs=pl.BlockSpec((1, tq, MLA_HPS * MLA_V), lambda bb, h, s, qi_r, kv_r: (bb, qi_r[s], h)),
        scratch_shapes=[
            pltpu.VMEM((MLA_HPS, tq, LANES), F32),
            pltpu.VMEM((MLA_HPS, tq, LANES), F32),
        ],
    )
    return pl.pallas_call(
        functools.partial(_mla_attn_kernel, tq=tq, tk=tk),
        grid_spec=grid_spec,
        out_shape=jax.ShapeDtypeStruct((b, t, MLA_HEADS * MLA_V), BF16),
        compiler_params=_cparams("parallel", "parallel", "arbitrary"),
        name="mla_attn",
    )(qi, kvt, q, kt, v)


def _mla_qabs_kernel(q_ref, wukt_ref, qa_ref):
    for hh in range(MLA_HEADS):
        qn = q_ref[:, hh * MLA_SEG:hh * MLA_SEG + MLA_NOPE]
        qa_ref[hh] = _dot(qn, wukt_ref[hh])


def _mla_qabs(q, wukt):
    m = q.shape[0]
    return pl.pallas_call(
        _mla_qabs_kernel,
        out_shape=jax.ShapeDtypeStruct((MLA_HEADS, m, MLA_KV_RANK), F32),
        name="mla_qabs",
    )(q, wukt)


def _mla_decode_kernel(pt_ref, qa_ref, qp_ref, latn_ref, kpen_ref, kinvn_ref, lc_ref, pc_ref, ic_ref,
                       o_ref, lbuf, pbuf, ibuf, sem, *, npages, layer):
    b = pl.program_id(0)
    nb = pl.num_programs(0)
    slot = b % 2

    def copies(seq, sl):
        out = []
        for p in range(npages):
            pg = pt_ref[seq * npages + p]
            rows = pl.ds(p * PAGE_SIZE, PAGE_SIZE)
            out.append(pltpu.make_async_copy(lc_ref.at[layer, pg], lbuf.at[sl, rows, :], sem.at[0, sl]))
            out.append(pltpu.make_async_copy(pc_ref.at[layer, pg], pbuf.at[sl, :, rows], sem.at[1, sl]))
            out.append(pltpu.make_async_copy(ic_ref.at[layer, pg], ibuf.at[sl, :, rows], sem.at[2, sl]))
        return out

    @pl.when(b == 0)
    def _():
        for cp in copies(0, 0):
            cp.start()

    @pl.when(b + 1 < nb)
    def _():
        for cp in copies(b + 1, 1 - slot):
            cp.start()

    for cp in copies(b, slot):
        cp.wait()

    qa = qa_ref[0].astype(BF16)
    qp = qp_ref[0]
    lat = lbuf[slot]
    sc = (_dot_nt(qa, lat) + _dot(qp, pbuf[slot])) * MLA_SCALE * ibuf[slot]
    latn = latn_ref[0]
    rnd = lambda a: a.astype(BF16).astype(F32)
    s_self = (jnp.sum(rnd(qa) * rnd(latn), axis=1, keepdims=True)
              + jnp.sum(rnd(qp) * rnd(kpen_ref[0]), axis=1, keepdims=True))
    s_self = s_self * MLA_SCALE * kinvn_ref[0]
    m = jnp.maximum(jnp.max(sc, axis=1, keepdims=True), s_self)
    e = jnp.exp(sc - m)
    e_self = jnp.exp(s_self - m)
    den = jnp.sum(e, axis=1, keepdims=True) + e_self
    ctx = _dot(e, lat) + e_self * latn
    o_ref[0] = ctx / den


def _mla_decode(qa, qp, lat_new, kpe_new, kinv_new, lat_pool, kpe_pool, kinv_pool, layer, page_table):
    nb = qa.shape[0]
    npages = page_table.shape[1]
    s = npages * PAGE_SIZE
    per_seq = lambda a: pl.BlockSpec((1,) + a.shape[1:], lambda b, pt: (b, 0, 0))
    grid_spec = pltpu.PrefetchScalarGridSpec(
        num_scalar_prefetch=1,
        grid=(nb,),
        in_specs=[per_seq(qa), per_seq(qp), per_seq(lat_new), per_seq(kpe_new), per_seq(kinv_new)]
        + [pl.BlockSpec(memory_space=pl.ANY)] * 3,
        out_specs=pl.BlockSpec((1, MLA_HEADS, MLA_KV_RANK), lambda b, pt: (b, 0, 0)),
        scratch_shapes=[
            pltpu.VMEM((2, s, MLA_KV_RANK), F32),
            pltpu.VMEM((2, MLA_ROPE, s), F32),
            pltpu.VMEM((2, MLA_HEADS, s), F32),
            pltpu.SemaphoreType.DMA((3, 2)),
        ],
    )
    return pl.pallas_call(
        functools.partial(_mla_decode_kernel, npages=npages, layer=layer),
        grid_spec=grid_spec,
        out_shape=jax.ShapeDtypeStruct((nb, MLA_HEADS, MLA_KV_RANK), F32),
        compiler_params=_cparams("arbitrary"),
        name="mla_decode",
    )(page_table.reshape(-1), qa, qp, lat_new, kpe_new, kinv_new, lat_pool, kpe_pool, kinv_pool)


def _mla_ctx_out_kernel(ctx_ref, wuv_ref, o_ref):
    for hh in range(MLA_HEADS):
        o_ref[:, hh * MLA_V:(hh + 1) * MLA_V] = _dot(ctx_ref[hh].astype(BF16), wuv_ref[hh]).astype(BF16)


def _mla_ctx_out(ctx_h, wuvh):
    m = ctx_h.shape[1]
    return pl.pallas_call(
        _mla_ctx_out_kernel,
        out_shape=jax.ShapeDtypeStruct((m, MLA_HEADS * MLA_V), BF16),
        name="mla_ctx_out",
    )(ctx_h, wuvh)


def kernel(x_prompt, x_sample, cache_sb_k, cache_sb_v, cache_mla_latent, cache_mla_kpe, cache_mla_kinv,
           state_s5_re, state_s5_im, page_table, ln_mix, ln_mlp, w_up, w_down,
           sb_w_qkv, sb_q_gain, sb_k_gain, sb_w_o,
           s5_a_re, s5_a_im, s5_log_dt, s5_b_re, s5_b_im, s5_c_re, s5_c_im, s5_d, s5_w_glu,
           mla_w_dq, mla_q_ln, mla_w_uq, mla_w_dkv, mla_kv_ln, mla_w_uk, mla_w_uv, mla_q_gain, mla_k_gain,
           mla_w_o):
    nb_p, t, d = x_prompt.shape
    nb_s, t_s, _ = x_sample.shape
    assert t_s == 1 and d == D_MODEL
    npages = page_table.shape[1]
    past_len = npages * PAGE_SIZE
    pos_p = jnp.arange(t, dtype=jnp.int32)
    pos_s = past_len + jnp.arange(t_s, dtype=jnp.int32)
    n_pool = cache_sb_k.shape[1]

    xp = x_prompt.reshape(nb_p * t, d)
    xs = x_sample.reshape(nb_s, d)
    outs = {k: [] for k in ("sbk_p", "sbv_p", "sbk_s", "sbv_s", "lat_p", "kpe_p", "kinv_p",
                            "lat_s", "kpe_s", "kinv_s", "s5r_p", "s5i_p", "s5r_s", "s5i_s")}
    sb_kc = jnp.transpose(cache_sb_k, (0, 1, 3, 4, 2)).reshape(-1, n_pool, SB_NKV, PAGE_SIZE)
    sb_vc = jnp.transpose(cache_sb_v, (0, 1, 3, 4, 2)).reshape(-1, n_pool, SB_NKV, PAGE_SIZE)
    mla_pc = jnp.transpose(cache_mla_kpe, (0, 1, 3, 2))
    mla_ic = jnp.transpose(cache_mla_kinv, (0, 1, 3, 2))
    sb_self_pos = jnp.concatenate([pos_s[-1:], pos_s[-1:]])
    kv_of_head = (jnp.arange(SB_HEADS)[:, None] // SB_GROUP == jnp.arange(SB_KV_HEADS)[None, :])

    for i in range(DEPTH):
        kind, j = i % N_MIXERS, i // N_MIXERS
        g_mix = ln_mix[i][None, :]
        if kind == 0:
            w = sb_w_qkv[j].astype(BF16)
            wo = sb_w_o[j].astype(BF16)
            q, _, _, kt, vt = _sb_proj(xp, g_mix, w, sb_q_gain[j], sb_k_gain[j], nb_p)
            outs["sbk_p"].append(jnp.transpose(kt.reshape(nb_p, SB_KV_HEADS, SB_HEAD_DIM, t), (0, 3, 1, 2)))
            outs["sbv_p"].append(jnp.transpose(vt.reshape(nb_p, SB_KV_HEADS, SB_HEAD_DIM, t), (0, 3, 1, 2)))
            o = _sb_attn(q.reshape(nb_p, t, SB_NQ), kt, vt)
            xp = _proj_res(xp, o.reshape(nb_p * t, SB_NQ), wo)

            q, k, v, _, _ = _sb_proj(xs, g_mix, w, sb_q_gain[j], sb_k_gain[j], 1)
            outs["sbk_s"].append(k.reshape(nb_s, 1, SB_KV_HEADS, SB_HEAD_DIM))
            outs["sbv_s"].append(v.reshape(nb_s, 1, SB_KV_HEADS, SB_HEAD_DIM))
            qbd = (q.reshape(nb_s, SB_HEADS, 1, SB_HEAD_DIM) * kv_of_head[None, :, :, None].astype(BF16))
            qbd = qbd.reshape(nb_s, SB_HEADS, SB_NKV)
            ot = _sb_decode(qbd, k, v, sb_kc, sb_vc, j, page_table, sb_self_pos)
            ot = ot.reshape(nb_s, SB_KV_HEADS, SB_HEAD_DIM, SB_KV_HEADS, SB_GROUP)
            o = jnp.einsum('bkdkg->bkgd', ot).reshape(nb_s, SB_NQ)
            xs = _proj_res(xs, o.astype(BF16), wo)
        elif kind == 1:
            abr, abi, bbr, bbi = _s5_disc(s5_a_re[j], s5_a_im[j], s5_log_dt[j], s5_b_re[j], s5_b_im[j])
            wts = _s5_block_weights(bbr, bbi, s5_c_re[j], s5_c_im[j])
            abr, abi = abr.reshape(1, S5_DIM), abi.reshape(1, S5_DIM)
            dd = s5_d[j][None, :]
            wglu = s5_w_glu[j].astype(BF16)
            zeros = jnp.zeros((nb_p, 1, S5_DIM), F32)
            y, sr, si = _s5(xp.reshape(nb_p, t, d), g_mix, wts, abr, abi, zeros, zeros, dd, wglu, True)
            xp = y.reshape(nb_p * t, d)
            outs["s5r_p"].append(sr.reshape(nb_p, S5_GROUPS, S5_STATE))
            outs["s5i_p"].append(si.reshape(nb_p, S5_GROUPS, S5_STATE))
            h0r = state_s5_re[j].reshape(1, nb_s, S5_DIM)
            h0i = state_s5_im[j].reshape(1, nb_s, S5_DIM)
            y, sr, si = _s5(xs.reshape(1, nb_s, d), g_mix, wts, abr, abi, h0r, h0i, dd, wglu, False)
            xs = y.reshape(nb_s, d)
            outs["s5r_s"].append(sr.reshape(nb_s, S5_GROUPS, S5_STATE))
            outs["s5i_s"].append(si.reshape(nb_s, S5_GROUPS, S5_STATE))
        else:
            mw = _mla_weights(mla_w_dq[j], mla_q_ln[j], mla_w_uq[j], mla_w_dkv[j], mla_kv_ln[j],
                              mla_w_uk[j], mla_w_uv[j], mla_q_gain[j], mla_k_gain[j])
            wo = mla_w_o[j].astype(BF16)
            pe = slice(MLA_NOPE, MLA_QK)
            q, lat, kpet, kinvt, kt, v = _mla_proj(xp, g_mix, mw, pos_p, nb_p)
            outs["lat_p"].append(lat.reshape(nb_p, t, MLA_KV_RANK))
            outs["kpe_p"].append(jnp.transpose(kpet[:, pe, :], (0, 2, 1)))
            outs["kinv_p"].append(jnp.transpose(kinvt, (0, 2, 1)))
            o = _mla_attn(q.reshape(nb_p, t, -1), kt, v.reshape(nb_p, t, -1))
            xp = _proj_res(xp, o.reshape(nb_p * t, -1), wo)

            q, lat, kpet, kinvt, _, _ = _mla_proj(xs, g_mix, mw, jnp.broadcast_to(pos_s, (nb_s,)), 1)
            kpe_s = jnp.transpose(kpet[0, pe, :], (1, 0))
            kinv_s = jnp.transpose(kinvt[0], (1, 0))
            outs["lat_s"].append(lat.reshape(nb_s, 1, MLA_KV_RANK))
            outs["kpe_s"].append(kpe_s.reshape(nb_s, 1, MLA_ROPE))
            outs["kinv_s"].append(kinv_s.reshape(nb_s, 1, MLA_HEADS))
            qa = jnp.transpose(_mla_qabs(q, mw["wukt"]), (1, 0, 2))
            qp = q.reshape(nb_s, MLA_HEADS, MLA_SEG)[:, :, pe]
            ctx = _mla_decode(qa, qp, lat[:, None, :], kpe_s[:, None, :], kinv_s[:, :, None],
                              cache_mla_latent, mla_pc, mla_ic, j, page_table)
            o = _mla_ctx_out(jnp.transpose(ctx, (1, 0, 2)), mw["wuvh"])
            xs = _proj_res(xs, o, wo)
        g_mlp = ln_mlp[i][None, :]
        wu = w_up[i].astype(BF16)
        wd = w_down[i].astype(BF16)
        xp = _mlp(xp, g_mlp, wu, wd)
        xs = _mlp(xs, g_mlp, wu, wd)

    st = lambda k: jnp.stack(outs[k])
    return (xp.reshape(nb_p, t, d), xs.reshape(nb_s, t_s, d),
            st("sbk_p"), st("sbv_p"), st("sbk_s"), st("sbv_s"),
            st("lat_p"), st("kpe_p"), st("kinv_p"), st("lat_s"), st("kpe_s"), st("kinv_s"),
            st("s5r_p"), st("s5i_p"), st("s5r_s"), st("s5i_s"))
```

```python
import functools
import math

import numpy as np
import jax
import jax.numpy as jnp
from jax import lax
from jax.experimental import pallas as pl
from jax.experimental.pallas import tpu as pltpu

F32 = jnp.float32
BF16 = jnp.bfloat16

D_MODEL = 1024
DEPTH = 4
PAGE_SIZE = 128
N_MIXERS = 3
D_FF = 4 * D_MODEL
SB_HEADS = 16
SB_KV_HEADS = 4
SB_HEAD_DIM = 64
SB_GROUP = SB_HEADS // SB_KV_HEADS
SB_SCALE = SB_HEAD_DIM ** -0.5
SB_NQ = SB_HEADS * SB_HEAD_DIM
SB_NKV = SB_KV_HEADS * SB_HEAD_DIM
S5_GROUP_CH = 16
S5_GROUPS = D_MODEL // S5_GROUP_CH
S5_STATE = 64
S5_DIM = S5_GROUPS * S5_STATE
MLA_HEADS = 16
MLA_Q_RANK = 384
MLA_KV_RANK = 256
MLA_NOPE = 64
MLA_ROPE = 32
MLA_V = 64
MLA_QK = MLA_NOPE + MLA_ROPE
MLA_SCALE = MLA_QK ** -0.5
ROPE_THETA = 10000.0
RMS_EPS = 1e-6
NEG_INF = -1e30

LANES = 128
MLA_SEG = LANES
VMEM_LIMIT_BYTES = 56 * 1024 * 1024


def _cparams(*sem):
    return pltpu.CompilerParams(dimension_semantics=sem, vmem_limit_bytes=VMEM_LIMIT_BYTES)


def _rms(x, g):
    xf = x.astype(F32)
    return xf * lax.rsqrt(jnp.mean(xf * xf, axis=-1, keepdims=True) + RMS_EPS) * g


def _dot(a, b):
    return jnp.dot(a, b, preferred_element_type=F32)


def _dot_nt(a, b):
    return lax.dot_general(a, b, (((1,), (1,)), ((), ())), preferred_element_type=F32)


def _split(x):
    hi = x.astype(BF16)
    lo = (x - hi.astype(F32)).astype(BF16)
    return hi, lo


def _sdot(x, s):
    hi, lo = _split(x)
    return _dot(hi, s) + _dot(lo, s)


def _sdot_nt(s, x):
    hi, lo = _split(x)
    return _dot_nt(s, hi) + _dot_nt(s, lo)


def _row_tile(m, want):
    t = min(m, want)
    assert m % t == 0
    return t


def _mlp_kernel(x_ref, g_ref, wu_ref, wd_ref, o_ref, h_ref, acc_ref):
    f = pl.program_id(1)

    @pl.when(f == 0)
    def _():
        h_ref[...] = _rms(x_ref[...], g_ref[...]).astype(BF16)
        acc_ref[...] = jnp.zeros_like(acc_ref)

    a = jnp.maximum(_dot(h_ref[...], wu_ref[...]), 0.0)
    acc_ref[...] += _dot((a * a).astype(BF16), wd_ref[...])

    @pl.when(f == pl.num_programs(1) - 1)
    def _():
        o_ref[...] = x_ref[...] + acc_ref[...]


def _mlp(x, g, wu, wd):
    m = x.shape[0]
    tm = _row_tile(m, 1024)
    tf = 512
    return pl.pallas_call(
        _mlp_kernel,
        grid=(m // tm, D_FF // tf),
        in_specs=[
            pl.BlockSpec((tm, D_MODEL), lambda i, f: (i, 0)),
            pl.BlockSpec((1, D_MODEL), lambda i, f: (0, 0)),
            pl.BlockSpec((D_MODEL, tf), lambda i, f: (0, f)),
            pl.BlockSpec((tf, D_MODEL), lambda i, f: (f, 0)),
        ],
        out_specs=pl.BlockSpec((tm, D_MODEL), lambda i, f: (i, 0)),
        out_shape=jax.ShapeDtypeStruct((m, D_MODEL), F32),
        scratch_shapes=[pltpu.VMEM((tm, D_MODEL), BF16), pltpu.VMEM((tm, D_MODEL), F32)],
        compiler_params=_cparams("parallel", "arbitrary"),
        name="mlp",
    )(x, g, wu, wd)


def _proj_res_kernel(x_ref, o_ref, w_ref, y_ref):
    y_ref[...] = x_ref[...] + _dot(o_ref[...], w_ref[...])


def _proj_res(x, o, w):
    m, k = o.shape
    tm = _row_tile(m, 1024)
    return pl.pallas_call(
        _proj_res_kernel,
        grid=(m // tm,),
        in_specs=[
            pl.BlockSpec((tm, D_MODEL), lambda i: (i, 0)),
            pl.BlockSpec((tm, k), lambda i: (i, 0)),
            pl.BlockSpec((k, D_MODEL), lambda i: (0, 0)),
        ],
        out_specs=pl.BlockSpec((tm, D_MODEL), lambda i: (i, 0)),
        out_shape=jax.ShapeDtypeStruct((m, D_MODEL), F32),
        compiler_params=_cparams("parallel"),
        name="proj_res",
    )(x, o, w)


def _seg_matrix(width, seg, cols=LANES):
    r = np.arange(width)[:, None] // seg
    c = np.arange(cols)[None, :]
    return jnp.asarray((r == c).astype(np.float32), dtype=BF16)


def _sb_proj_kernel(x_ref, g_ref, w_ref, seg_ref, segt_ref, gain_ref, q_ref, k_ref, v_ref, kt_ref, vt_ref):
    h = _rms(x_ref[...], g_ref[...]).astype(BF16)
    qkv = _dot(h, w_ref[...])
    nqk = SB_NQ + SB_NKV
    qk = qkv[:, :nqk]
    ms = _sdot(qk * qk, seg_ref[...]) * (1.0 / SB_HEAD_DIM)
    scale = _sdot(lax.rsqrt(ms + RMS_EPS), segt_ref[...])
    qkn = qk * scale * gain_ref[...]
    kn = qkn[:, SB_NQ:]
    v = qkv[:, nqk:]
    q_ref[...] = qkn[:, :SB_NQ].astype(BF16)
    k_ref[...] = kn
    v_ref[...] = v
    kt_ref[0] = kn.T
    vt_ref[0] = v.T


def _sb_proj(x, g, w, gq, gk, nseq):
    m = x.shape[0]
    t = m // nseq
    tm = _row_tile(t, 512)
    nt = t // tm
    nqk = SB_NQ + SB_NKV
    seg = _seg_matrix(nqk, SB_HEAD_DIM)
    gain = jnp.concatenate([jnp.tile(gq * SB_SCALE, SB_HEADS), jnp.tile(gk, SB_KV_HEADS)])[None, :]
    return pl.pallas_call(
        _sb_proj_kernel,
        grid=(m // tm,),
        in_specs=[
            pl.BlockSpec((tm, D_MODEL), lambda i: (i, 0)),
            pl.BlockSpec((1, D_MODEL), lambda i: (0, 0)),
            pl.BlockSpec(w.shape, lambda i: (0, 0)),
            pl.BlockSpec(seg.shape, lambda i: (0, 0)),
            pl.BlockSpec(seg.T.shape, lambda i: (0, 0)),
            pl.BlockSpec(gain.shape, lambda i: (0, 0)),
        ],
        out_specs=[
            pl.BlockSpec((tm, SB_NQ), lambda i: (i, 0)),
            pl.BlockSpec((tm, SB_NKV), lambda i: (i, 0)),
            pl.BlockSpec((tm, SB_NKV), lambda i: (i, 0)),
            pl.BlockSpec((1, SB_NKV, tm), lambda i: (i // nt, 0, i % nt)),
            pl.BlockSpec((1, SB_NKV, tm), lambda i: (i // nt, 0, i % nt)),
        ],
        out_shape=[
            jax.ShapeDtypeStruct((m, SB_NQ), BF16),
            jax.ShapeDtypeStruct((m, SB_NKV), F32),
            jax.ShapeDtypeStruct((m, SB_NKV), F32),
            jax.ShapeDtypeStruct((nseq, SB_NKV, t), F32),
            jax.ShapeDtypeStruct((nseq, SB_NKV, t), F32),
        ],
        compiler_params=_cparams("parallel"),
        name="sb_proj",
    )(x, g, w, seg, seg.T, gain)


def _tri_ones():
    j = np.arange(LANES)[:, None]
    s = np.arange(LANES)[None, :]
    u = np.concatenate([(j > s).astype(np.float32), np.ones((LANES, LANES), np.float32)], axis=1)
    return jnp.asarray(np.concatenate([u, u], axis=0), dtype=BF16)


def _log_sigmoid_neg(z):
    return jnp.minimum(-z, 0.0) - jnp.log(1.0 + jnp.exp(-jnp.abs(z)))


SB_CUM = 2 * LANES
LOG2E = math.log2(math.e)


def _tri_upper():
    j = np.arange(SB_CUM)[:, None]
    s = np.arange(SB_CUM)[None, :]
    return jnp.asarray((j > s).astype(np.float32), dtype=BF16)


SB_DEAD = 110.0
SB_KEY_BLOCK = 4


def _sb_attn_kernel(qi_ref, kb_ref, first_ref, q_ref, kt_ref, vt_ref, u_ref, o_ref, acc_ref, c_ref, dead_ref,
                    *, tq, nkb):
    s = pl.program_id(2)
    i = qi_ref[s]
    kb = kb_ref[s]
    tk = tq

    @pl.when(first_ref[s] == 1)
    def _():
        acc_ref[...] = jnp.zeros_like(acc_ref)
        c_ref[...] = jnp.zeros_like(c_ref)
        dead_ref[0] = 0

    def step(masked, kv, lo):
        kt = kt_ref[0, :, pl.ds(lo, tk)].astype(BF16)
        vt = vt_ref[0, :, pl.ds(lo, tk)].astype(BF16)
        u = u_ref[...]
        if masked:
            row = lax.broadcasted_iota(jnp.int32, (tq, tk), 0) + i * tq
            col = lax.broadcasted_iota(jnp.int32, (tq, tk), 1) + kv * tk
            valid = col < row
        for g in range(SB_GROUP):
            qg = q_ref[0, :, g * SB_HEAD_DIM:(g + 1) * SB_HEAD_DIM]
            z = _dot(qg, kt)
            sp = jnp.maximum(z, 0.0) + jnp.log(1.0 + jnp.exp2(jnp.abs(z) * (-LOG2E)))
            if masked:
                sp = jnp.where(valid, sp, 0.0)
            spb = sp.astype(BF16)
            c = c_ref[g]
            nsub = tk // SB_CUM
            tails = [None] * nsub
            for sb in reversed(range(nsub)):
                blk = spb[:, sb * SB_CUM:(sb + 1) * SB_CUM]
                later = _dot(blk, u)
                tails[sb] = later + jnp.concatenate([c, c], axis=1)
                c = c + (later[:, :1] + blk[:, :1].astype(F32))
            c_ref[g] = c
            w = jnp.exp(z - sp - jnp.concatenate(tails, axis=1))
            if masked:
                w = jnp.where(valid, w, 0.0)
            acc_ref[g] += _dot_nt(w.astype(BF16), vt)
        cmin = functools.reduce(jnp.minimum, [c_ref[g] for g in range(SB_GROUP)])
        dead_ref[0] = (jnp.min(cmin) >= SB_DEAD).astype(jnp.int32)

    def tile_step(j, carry):
        sub = nkb - 1 - j
        kv = kb * nkb + sub
        lo = pl.multiple_of(sub * tk, tk)
        live = dead_ref[0] == 0

        @pl.when(jnp.logical_and(kv == i, live))
        def _():
            step(True, kv, lo)

        @pl.when(jnp.logical_and(kv < i, live))
        def _():
            step(False, kv, lo)

        return carry

    lax.fori_loop(0, nkb, tile_step, 0)

    @pl.when(kb == 0)
    def _():
        o_ref[0] = jnp.concatenate([acc_ref[g] for g in range(SB_GROUP)], axis=1).astype(BF16)


def _sb_pairs(nq, nkb):
    qi, kb, first = [], [], []
    for i in range(nq):
        blocks = list(range(i // nkb, -1, -1))
        qi += [i] * len(blocks)
        kb += blocks
        first += [1] + [0] * (len(blocks) - 1)
    return tuple(jnp.asarray(a, jnp.int32) for a in (qi, kb, first))


def _causal_pairs(nq, r, reverse):
    qi, kv = [], []
    for i in range(nq):
        ks = list(range((i + 1) * r))
        if reverse:
            ks = ks[::-1]
        qi += [i] * len(ks)
        kv += ks
    return jnp.asarray(qi, jnp.int32), jnp.asarray(kv, jnp.int32)


def _sb_attn(q, kt, vt):
    b, t, _ = q.shape
    tq = _row_tile(t, 512)
    nq = t // tq
    nkb = min(SB_KEY_BLOCK, nq)
    assert nq % nkb == 0
    qi, kb, first = _sb_pairs(nq, nkb)
    gw = SB_GROUP * SB_HEAD_DIM
    uo = _tri_upper()
    grid_spec = pltpu.PrefetchScalarGridSpec(
        num_scalar_prefetch=3,
        grid=(b, SB_KV_HEADS, int(qi.shape[0])),
        in_specs=[
            pl.BlockSpec((1, tq, gw), lambda bb, h, s, qi_r, kb_r, f_r: (bb, qi_r[s], h)),
            pl.BlockSpec((1, SB_HEAD_DIM, nkb * tq), lambda bb, h, s, qi_r, kb_r, f_r: (bb, h, kb_r[s])),
            pl.BlockSpec((1, SB_HEAD_DIM, nkb * tq), lambda bb, h, s, qi_r, kb_r, f_r: (bb, h, kb_r[s])),
            pl.BlockSpec(uo.shape, lambda bb, h, s, qi_r, kb_r, f_r: (0, 0)),
        ],
        out_specs=pl.BlockSpec((1, tq, gw), lambda bb, h, s, qi_r, kb_r, f_r: (bb, qi_r[s], h)),
        scratch_shapes=[
            pltpu.VMEM((SB_GROUP, tq, SB_HEAD_DIM), F32),
            pltpu.VMEM((SB_GROUP, tq, LANES), F32),
            pltpu.SMEM((1,), jnp.int32),
        ],
    )
    return pl.pallas_call(
        functools.partial(_sb_attn_kernel, tq=tq, nkb=nkb),
        grid_spec=grid_spec,
        out_shape=jax.ShapeDtypeStruct((b, t, SB_NQ), BF16),
        compiler_params=_cparams("parallel", "parallel", "arbitrary"),
        name="sb_attn",
    )(qi, kb, first, q, kt, vt, uo)


SB_DEC_WINDOW = 4


def _page_copies(pt_ref, seq, pages, dsts, sems, *, npages, layer, pools):
    out = []
    for pool, dst, sem in zip(pools, dsts, sems):
        for k, p in enumerate(pages):
            pg = pt_ref[seq * npages + p]
            out.append(pltpu.make_async_copy(
                pool.at[layer, pg], dst.at[:, pl.ds(k * PAGE_SIZE, PAGE_SIZE)], sem))
    return out


def _sb_decode_kernel(pt_ref, vs_ref, qbd_ref, qbdt_ref, knr_ref, knc_ref, vnc_ref, uo_ref, kc_ref, vc_ref,
                      o_ref, kwin, vwin, kold, vold, sem, sem_old, *, npages, layer, nwin):
    b = pl.program_id(0)
    nb = pl.num_programs(0)
    slot = b % 2
    nold = npages - nwin
    copies = functools.partial(_page_copies, pt_ref, npages=npages, layer=layer, pools=(kc_ref, vc_ref))
    win = lambda seq, sl: copies(seq, range(nold, npages), (kwin.at[sl], vwin.at[sl]), (sem.at[0, sl], sem.at[1, sl]))
    old = lambda seq: copies(seq, range(nold), (kold, vold), (sem_old.at[0], sem_old.at[1]))

    @pl.when(b == 0)
    def _():
        for cp in win(0, 0):
            cp.start()

    @pl.when(b + 1 < nb)
    def _():
        for cp in win(b + 1, 1 - slot):
            cp.start()

    for cp in win(b, slot):
        cp.wait()

    qbd = qbd_ref[0]
    nh = qbd.shape[0]
    self_valid = vs_ref[0] < vs_ref[1]
    z_col = jnp.sum(qbd.astype(F32) * knr_ref[0], axis=1, keepdims=True)
    l_col = jnp.where(self_valid, _log_sigmoid_neg(z_col), 0.0)
    z_row = jnp.sum(qbdt_ref[0].astype(F32) * knc_ref[0], axis=0, keepdims=True)
    w_row = jnp.where(self_valid, jnp.exp(z_row + _log_sigmoid_neg(z_row)), 0.0)

    def attend(kt, vt, n, c):
        z_all = _dot(qbd, kt)
        z = jnp.concatenate([z_all[:, p * PAGE_SIZE:(p + 1) * PAGE_SIZE] for p in range(n)], axis=0)
        lneg = _log_sigmoid_neg(z)
        hi, lo = _split(lneg)
        res = _dot(jnp.concatenate([hi, lo], axis=1), uo_ref[...])
        tails = [None] * n
        for p in reversed(range(n)):
            rs = slice(p * nh, (p + 1) * nh)
            tails[p] = res[rs, :PAGE_SIZE] + c
            c = c + res[rs, PAGE_SIZE:]
        w = jnp.exp(z + lneg + jnp.concatenate(tails, axis=0))
        w_all = jnp.concatenate([w[p * nh:(p + 1) * nh] for p in range(n)], axis=1)
        return _dot_nt(vt, w_all), c

    ot, c = attend(kwin[slot], vwin[slot], nwin, jnp.broadcast_to(l_col, (nh, PAGE_SIZE)))
    o_ref[0] = ot + vnc_ref[0] * w_row

    if nold:
        @pl.when(jnp.max(c) > -SB_DEAD)
        def _():
            for cp in old(b):
                cp.start()
            for cp in old(b):
                cp.wait()
            o_ref[0] += attend(kold[...], vold[...], nold, c)[0]


def _sb_decode(qbd, k_new, v_new, kc, vc, layer, page_table, self_pos):
    nb, nh, kw = qbd.shape
    npages = page_table.shape[1]
    nwin = min(SB_DEC_WINDOW, npages)
    nold = npages - nwin
    uo = _tri_ones()
    im = lambda b, pt, vs: (b, 0, 0)
    grid_spec = pltpu.PrefetchScalarGridSpec(
        num_scalar_prefetch=2,
        grid=(nb,),
        in_specs=[
            pl.BlockSpec((1, nh, kw), im),
            pl.BlockSpec((1, kw, nh), im),
            pl.BlockSpec((1, 1, kw), im),
            pl.BlockSpec((1, kw, 1), im),
            pl.BlockSpec((1, kw, 1), im),
            pl.BlockSpec(uo.shape, lambda b, pt, vs: (0, 0)),
            pl.BlockSpec(memory_space=pl.ANY),
            pl.BlockSpec(memory_space=pl.ANY),
        ],
        out_specs=pl.BlockSpec((1, kw, nh), im),
        scratch_shapes=[
            pltpu.VMEM((2, kw, nwin * PAGE_SIZE), F32),
            pltpu.VMEM((2, kw, nwin * PAGE_SIZE), F32),
            pltpu.VMEM((kw, max(nold, 1) * PAGE_SIZE), F32),
            pltpu.VMEM((kw, max(nold, 1) * PAGE_SIZE), F32),
            pltpu.SemaphoreType.DMA((2, 2)),
            pltpu.SemaphoreType.DMA((2,)),
        ],
    )
    return pl.pallas_call(
        functools.partial(_sb_decode_kernel, npages=npages, layer=layer, nwin=nwin),
        grid_spec=grid_spec,
        out_shape=jax.ShapeDtypeStruct((nb, kw, nh), F32),
        compiler_params=_cparams("arbitrary"),
        name="sb_decode",
    )(page_table.reshape(-1), self_pos, qbd, jnp.transpose(qbd, (0, 2, 1)), k_new[:, None, :],
      k_new[:, :, None], v_new[:, :, None], uo, kc, vc)


def _s5_disc_kernel(are_ref, aim_ref, ldt_ref, arex_ref, aimx_ref, ldtx_ref, bre_ref, bim_ref,
                    abr_ref, abi_ref, bbr_ref, bbi_ref):
    def zoh(a_re, a_im, log_dt):
        dt = jnp.exp(log_dt)
        mag = jnp.exp(dt * a_re)
        ab_re = mag * jnp.cos(dt * a_im)
        ab_im = mag * jnp.sin(dt * a_im)
        den = a_re * a_re + a_im * a_im
        xr, yi = ab_re - 1.0, ab_im
        return ab_re, ab_im, (xr * a_re + yi * a_im) / den, (yi * a_re - xr * a_im) / den

    ab_re, ab_im, _, _ = zoh(are_ref[...], aim_ref[...], ldt_ref[...])
    abr_ref[...] = ab_re
    abi_ref[...] = ab_im
    _, _, c_re, c_im = zoh(arex_ref[...], aimx_ref[...], ldtx_ref[...])
    b_re = bre_ref[...]
    b_im = bim_ref[...]
    bbr_ref[...] = c_re * b_re - c_im * b_im
    bbi_ref[...] = c_re * b_im + c_im * b_re


def _s5_disc(a_re, a_im, log_dt, b_re, b_im):
    g, p = a_re.shape
    c = b_re.shape[-1]
    rep = lambda a: jnp.repeat(a, c, axis=1)
    ldt = jnp.broadcast_to(log_dt[:, None], (g, p))
    args = (a_re, a_im, ldt, rep(a_re), rep(a_im), rep(ldt), b_re.reshape(g, p * c), b_im.reshape(g, p * c))
    small = jax.ShapeDtypeStruct((g, p), F32)
    wide = jax.ShapeDtypeStruct((g, p * c), F32)
    return pl.pallas_call(
        _s5_disc_kernel,
        out_shape=[small, small, wide, wide],
        name="s5_disc",
    )(*args)


def _s5_block_weights(bb_re, bb_im, c_re, c_im):
    g, c, p = S5_GROUPS, S5_GROUP_CH, S5_STATE
    nblk = 4
    gl = g // nblk
    eye = jnp.eye(gl, dtype=F32)

    def in_blk(bb):
        bb = bb.reshape(nblk, gl, p, c)
        return jnp.einsum('kgpc,gh->kgchp', bb, eye).reshape(nblk, gl * c, gl * p).astype(BF16)

    def out_blk(cc):
        cc = cc.reshape(nblk, gl, c, p)
        return jnp.einsum('kgcp,gh->kgphc', cc, eye).reshape(nblk, gl * p, gl * c).astype(BF16)

    return in_blk(bb_re), in_blk(bb_im), out_blk(c_re), out_blk(c_im)


def _s5_kernel(x_ref, g_ref, bre_ref, bim_ref, abr_ref, abi_ref, h0r_ref, h0i_ref, cre_ref, cim_ref,
               d_ref, wg_ref, o_ref, sr_ref, si_ref, str_ref, sti_ref, hr_ref, hi_ref, *, sequential):
    nblk = bre_ref.shape[0]
    cw = D_MODEL // nblk
    sw = S5_DIM // nblk
    x = x_ref[0]
    tt = x.shape[0]
    u = _rms(x, g_ref[...])
    ub = u.astype(BF16)
    for k in range(nblk):
        uk = ub[:, k * cw:(k + 1) * cw]
        str_ref[:, k * sw:(k + 1) * sw] = _dot(uk, bre_ref[k])
        sti_ref[:, k * sw:(k + 1) * sw] = _dot(uk, bim_ref[k])

    if sequential:
        @pl.when(pl.program_id(1) == 0)
        def _():
            hr_ref[...] = h0r_ref[0]
            hi_ref[...] = h0i_ref[0]

        for k in range(nblk):
            cs = slice(k * sw, (k + 1) * sw)
            ar = abr_ref[:, cs]
            ai = abi_ref[:, cs]

            def body(r, carry):
                hr, hi = carry
                br = str_ref[pl.ds(r, 1), cs]
                bi = sti_ref[pl.ds(r, 1), cs]
                nr = ar * hr - ai * hi + br
                ni = ar * hi + ai * hr + bi
                str_ref[pl.ds(r, 1), cs] = nr
                sti_ref[pl.ds(r, 1), cs] = ni
                return nr, ni

            hr, hi = lax.fori_loop(0, tt, body, (hr_ref[:, cs], hi_ref[:, cs]))
            hr_ref[:, cs] = hr
            hi_ref[:, cs] = hi
        sr_ref[0] = hr_ref[...]
        si_ref[0] = hi_ref[...]
    else:
        ar = abr_ref[...]
        ai = abi_ref[...]
        h0r = h0r_ref[0]
        h0i = h0i_ref[0]
        nr = ar * h0r - ai * h0i + str_ref[...]
        ni = ar * h0i + ai * h0r + sti_ref[...]
        str_ref[...] = nr
        sti_ref[...] = ni
        sr_ref[0] = nr
        si_ref[0] = ni

    ys = []
    for k in range(nblk):
        cs = slice(k * sw, (k + 1) * sw)
        ys.append(_dot(str_ref[:, cs].astype(BF16), cre_ref[k]) - _dot(sti_ref[:, cs].astype(BF16), cim_ref[k]))
    y = jnp.concatenate(ys, axis=1) + d_ref[...] * u
    y = jax.nn.gelu(y)
    vg = _dot(y.astype(BF16), wg_ref[...])
    o_ref[0] = x + vg[:, :D_MODEL] * jax.nn.sigmoid(vg[:, D_MODEL:])


def _s5(x, g, wts, abr, abi, h0r, h0i, d, wglu, sequential):
    b, t, _ = x.shape
    tt = _row_tile(t, 256)
    bre, bim, cre, cim = wts
    srow = 1 if sequential else tt
    st_spec = pl.BlockSpec((1, srow, S5_DIM), (lambda bb, i: (bb, 0, 0)) if sequential else (lambda bb, i: (bb, i, 0)))
    full = lambda a: pl.BlockSpec(a.shape, lambda bb, i: (0,) * a.ndim)
    return pl.pallas_call(
        functools.partial(_s5_kernel, sequential=sequential),
        grid=(b, t // tt),
        in_specs=[
            pl.BlockSpec((1, tt, D_MODEL), lambda bb, i: (bb, i, 0)),
            full(g), full(bre), full(bim), full(abr), full(abi),
            st_spec, st_spec,
            full(cre), full(cim), full(d), full(wglu),
        ],
        out_specs=[pl.BlockSpec((1, tt, D_MODEL), lambda bb, i: (bb, i, 0)), st_spec, st_spec],
        out_shape=[
            jax.ShapeDtypeStruct((b, t, D_MODEL), F32),
            jax.ShapeDtypeStruct((b, t if not sequential else 1, S5_DIM), F32),
            jax.ShapeDtypeStruct((b, t if not sequential else 1, S5_DIM), F32),
        ],
        scratch_shapes=[
            pltpu.VMEM((tt, S5_DIM), F32), pltpu.VMEM((tt, S5_DIM), F32),
            pltpu.VMEM((1, S5_DIM), F32), pltpu.VMEM((1, S5_DIM), F32),
        ],
        compiler_params=_cparams("parallel", "arbitrary"),
        name="s5_seq" if sequential else "s5_step",
    )(x, g, bre, bim, abr, abi, h0r, h0i, cre, cim, d, wglu)


def _seg_cols(w_nope, w_pe):
    k, h, _ = w_nope.shape
    w_pe = jnp.broadcast_to(w_pe, (k, h, MLA_ROPE))
    pad = jnp.zeros((k, h, MLA_SEG - MLA_QK), w_nope.dtype)
    return jnp.concatenate([w_nope, w_pe, pad], axis=-1).reshape(k, h * MLA_SEG)


def _seg_vec(v_nope, v_pe, fill=0.0):
    pad = jnp.full((MLA_SEG - MLA_QK,), fill, F32)
    return jnp.concatenate([v_nope, v_pe, pad])[None, :]


def _rope_tables(pos):
    half = MLA_ROPE // 2
    freqs = ROPE_THETA ** (-jnp.arange(half, dtype=F32) / half)
    ang = pos.astype(F32)[:, None] * freqs[None, :]
    cos, sin = jnp.cos(ang), jnp.sin(ang)
    n = pos.shape[0]
    one = jnp.ones((n, MLA_NOPE), F32)
    zero = jnp.zeros((n, MLA_NOPE), F32)
    z32 = jnp.zeros((n, MLA_SEG - MLA_QK), F32)
    z16 = jnp.zeros((n, half), F32)
    c = jnp.concatenate([one, cos, cos, z32], axis=1)
    s_lo = jnp.concatenate([zero, -sin, z16, z32], axis=1)
    s_hi = jnp.concatenate([zero, z16, sin, z32], axis=1)
    return c, s_lo, s_hi


def _rope_seg(x, c, s_lo, s_hi):
    half = MLA_ROPE // 2
    up = pltpu.roll(x, MLA_SEG - half, 1)
    dn = pltpu.roll(x, half, 1)
    return x * c + up * s_lo + dn * s_hi


def _mla_proj_kernel(x_ref, g_ref, wdq_ref, gql_ref, wuq_ref, gq_ref, gkq_ref, wdkv_ref, gkvl_ref,
                     wuk_ref, wuv_ref, vone_ref, gkpe_ref, segt_ref, c_ref, slo_ref, shi_ref,
                     q_ref, lat_ref, kpe_ref, kinvt_ref, kt_ref, v_ref):
    h = _rms(x_ref[...], g_ref[...]).astype(BF16)
    c, s_lo, s_hi = c_ref[...], slo_ref[...], shi_ref[...]
    cq = _rms(_dot(h, wdq_ref[...]), gql_ref[...]).astype(BF16)
    qraw = _dot(cq, wuq_ref[...])
    for hh in range(MLA_HEADS):
        qs = qraw[:, hh * MLA_SEG:(hh + 1) * MLA_SEG]
        qs = qs * lax.rsqrt(jnp.sum(qs * qs, axis=1, keepdims=True) * (1.0 / MLA_QK) + RMS_EPS) * gq_ref[...]
        qs = _rope_seg(qs * gkq_ref[...], c, s_lo, s_hi)
        q_ref[:, hh * MLA_SEG:(hh + 1) * MLA_SEG] = qs.astype(BF16)
    ckv = _dot(h, wdkv_ref[...])
    lat = _rms(ckv[:, :MLA_KV_RANK], gkvl_ref[...])
    lat_ref[...] = lat
    latb = lat.astype(BF16)
    kpe_raw = ckv[:, MLA_KV_RANK:]
    kn = _dot(latb, wuk_ref[...])
    ones = jnp.ones((8, MLA_SEG), BF16)
    pe_ss = _sdot_nt(ones, kpe_raw * kpe_raw)[:1]
    ms = (_sdot_nt(segt_ref[...], kn * kn) + pe_ss) * (1.0 / MLA_QK)
    kinvt = lax.rsqrt(ms + RMS_EPS)
    kinvt_ref[0] = kinvt
    kpe = _rope_seg(kpe_raw * gkpe_ref[...], c, s_lo, s_hi)
    kpe_ref[0] = kpe.T
    for hh in range(MLA_HEADS):
        ks = (kn[:, hh * MLA_SEG:(hh + 1) * MLA_SEG] + kpe).T
        kt_ref[0, hh] = (ks * (kinvt[hh:hh + 1] * (MLA_SCALE * LOG2E))).astype(BF16)
    v_ref[...] = (_dot(latb, wuv_ref[...]) + vone_ref[...]).astype(BF16)


def _mla_weights(w_dq, g_ql, w_uq, w_dkv, g_kvl, w_uk, w_uv, g_q, g_k):
    half = MLA_ROPE // 2
    uq = w_uq.reshape(MLA_Q_RANK, MLA_HEADS, MLA_QK)
    wuq = _seg_cols(uq[..., :MLA_NOPE], uq[..., MLA_NOPE:]).astype(BF16)
    kvn = jnp.zeros((D_MODEL, 1, MLA_NOPE), F32)
    wdkv = jnp.concatenate([w_dkv[:, :MLA_KV_RANK], _seg_cols(kvn, w_dkv[:, None, MLA_KV_RANK:])], axis=1).astype(BF16)
    wuk = _seg_cols(w_uk, jnp.zeros((MLA_KV_RANK, 1, MLA_ROPE), F32)).astype(BF16)
    return dict(
        wdq=w_dq.astype(BF16), gql=g_ql[None, :], wuq=wuq,
        gq=_seg_vec(g_q[:MLA_NOPE], g_q[MLA_NOPE:]),
        gkq=_seg_vec(g_k[:MLA_NOPE], jnp.ones((MLA_ROPE,), F32)),
        wdkv=wdkv, gkvl=g_kvl[None, :], wuk=wuk,
        wuv=jnp.concatenate([w_uv, jnp.zeros_like(w_uv)], axis=-1).reshape(MLA_KV_RANK, -1).astype(BF16),
        vone=jnp.tile(jnp.concatenate([jnp.zeros((MLA_V,), F32), jnp.ones((MLA_SEG - MLA_V,), F32)]),
                      MLA_HEADS)[None, :],
        gkpe=_seg_vec(jnp.zeros((MLA_NOPE,), F32), g_k[MLA_NOPE:]),
        segt=_seg_matrix(MLA_HEADS * MLA_SEG, MLA_SEG, cols=MLA_HEADS).T,
        wukt=jnp.transpose(w_uk, (1, 2, 0)).astype(BF16),
        wuvh=jnp.transpose(w_uv, (1, 0, 2)).astype(BF16),
    )


def _mla_proj(x, g, mw, pos, nseq):
    m = x.shape[0]
    t = m // nseq
    tm = _row_tile(t, 256)
    nt = t // tm
    tabs = _rope_tables(pos)
    hs = MLA_HEADS * MLA_SEG
    full = lambda a: pl.BlockSpec(a.shape, lambda i: (0,) * a.ndim)
    tab_spec = pl.BlockSpec((tm, MLA_SEG), lambda i: (i % nt, 0))
    tmaj = lambda rows: pl.BlockSpec((1, rows, tm), lambda i: (i // nt, 0, i % nt))
    ws = [mw[k] for k in ("wdq", "gql", "wuq", "gq", "gkq", "wdkv", "gkvl", "wuk", "wuv", "vone", "gkpe", "segt")]
    return pl.pallas_call(
        _mla_proj_kernel,
        grid=(m // tm,),
        in_specs=[pl.BlockSpec((tm, D_MODEL), lambda i: (i, 0)), full(g)] + [full(w) for w in ws]
        + [tab_spec] * 3,
        out_specs=[
            pl.BlockSpec((tm, hs), lambda i: (i, 0)),
            pl.BlockSpec((tm, MLA_KV_RANK), lambda i: (i, 0)),
            tmaj(MLA_SEG),
            tmaj(MLA_HEADS),
            pl.BlockSpec((1, MLA_HEADS, MLA_SEG, tm), lambda i: (i // nt, 0, 0, i % nt)),
            pl.BlockSpec((tm, hs), lambda i: (i, 0)),
        ],
        out_shape=[
            jax.ShapeDtypeStruct((m, hs), BF16),
            jax.ShapeDtypeStruct((m, MLA_KV_RANK), F32),
            jax.ShapeDtypeStruct((nseq, MLA_SEG, t), F32),
            jax.ShapeDtypeStruct((nseq, MLA_HEADS, t), F32),
            jax.ShapeDtypeStruct((nseq, MLA_HEADS, MLA_SEG, t), BF16),
            jax.ShapeDtypeStruct((m, hs), BF16),
        ],
        compiler_params=_cparams("parallel"),
        name="mla_proj",
    )(x, g, *ws, *tabs)


MLA_HPS = 4


def _mla_attn_kernel(qi_ref, kv_ref, q_ref, kt_ref, v_ref, o_ref, m_ref, acc_ref, *, tq, tk):
    s = pl.program_id(2)
    i = qi_ref[s]
    kv = kv_ref[s]
    r = tq // tk

    @pl.when(kv == 0)
    def _():
        m_ref[...] = jnp.full_like(m_ref, NEG_INF)
        acc_ref[...] = jnp.zeros_like(acc_ref)

    def step(masked):
        if masked:
            row = lax.broadcasted_iota(jnp.int32, (tq, tk), 0) + i * tq
            col = lax.broadcasted_iota(jnp.int32, (tq, tk), 1) + kv * tk
            valid = col <= row
        for hh in range(MLA_HPS):
            sc = _dot(q_ref[0, :, hh * MLA_SEG:(hh + 1) * MLA_SEG], kt_ref[0, hh])
            if masked:
                sc = jnp.where(valid, sc, NEG_INF)
            m_prev = m_ref[hh]
            m_new = jnp.maximum(m_prev, jnp.max(sc, axis=1, keepdims=True))
            p = jnp.exp2(sc - m_new[:, :1])
            pv = _dot(p.astype(BF16), v_ref[0, :, hh * MLA_SEG:(hh + 1) * MLA_SEG])
            acc_ref[hh] = acc_ref[hh] * jnp.exp2(m_prev - m_new) + pv
            m_ref[hh] = m_new

    diag = kv >= i * r

    @pl.when(diag)
    def _():
        step(True)

    @pl.when(jnp.logical_not(diag))
    def _():
        step(False)

    @pl.when(kv == (i + 1) * r - 1)
    def _():
        outs = []
        for hh in range(MLA_HPS):
            a = acc_ref[hh]
            outs.append((a / pltpu.roll(a, MLA_V, 1))[:, :MLA_V])
        o_ref[0] = jnp.concatenate(outs, axis=1).astype(BF16)


def _mla_attn(q, kt, v):
    b, t, _ = q.shape
    tq = _row_tile(t, 512)
    tk = tq
    qi, kvt = _causal_pairs(t // tq, tq // tk, reverse=False)
    grid_spec = pltpu.PrefetchScalarGridSpec(
        num_scalar_prefetch=2,
        grid=(b, MLA_HEADS // MLA_HPS, int(qi.shape[0])),
        in_specs=[
            pl.BlockSpec((1, tq, MLA_HPS * MLA_SEG), lambda bb, h, s, qi_r, kv_r: (bb, qi_r[s], h)),
            pl.BlockSpec((1, MLA_HPS, MLA_SEG, tk), lambda bb, h, s, qi_r, kv_r: (bb, h, 0, kv_r[s])),
            pl.BlockSpec((1, tk, MLA_HPS * MLA_SEG), lambda bb, h, s, qi_r, kv_r: (bb, kv_r[s], h)),
        ],
        out_specs=pl.BlockSpec((1, tq, MLA_HPS * MLA_V), lambda bb, h, s, qi_r, kv_r: (bb, qi_r[s], h)),
        scratch_shapes=[
            pltpu.VMEM((MLA_HPS, tq, LANES), F32),
            pltpu.VMEM((MLA_HPS, tq, LANES), F32),
        ],
    )
    return pl.pallas_call(
        functools.partial(_mla_attn_kernel, tq=tq, tk=tk),
        grid_spec=grid_spec,
        out_shape=jax.ShapeDtypeStruct((b, t, MLA_HEADS * MLA_V), BF16),
        compiler_params=_cparams("parallel", "parallel", "arbitrary"),
        name="mla_attn",
    )(qi, kvt, q, kt, v)


def _mla_qabs_kernel(q_ref, wukt_ref, qa_ref):
    for hh in range(MLA_HEADS):
        qn = q_ref[:, hh * MLA_SEG:hh * MLA_SEG + MLA_NOPE]
        qa_ref[hh] = _dot(qn, wukt_ref[hh])


def _mla_qabs(q, wukt):
    m = q.shape[0]
    return pl.pallas_call(
        _mla_qabs_kernel,
        out_shape=jax.ShapeDtypeStruct((MLA_HEADS, m, MLA_KV_RANK), F32),
        name="mla_qabs",
    )(q, wukt)


def _mla_decode_kernel(pt_ref, qa_ref, qp_ref, latn_ref, kpen_ref, kinvn_ref, lc_ref, pc_ref, ic_ref,
                       o_ref, lbuf, pbuf, ibuf, sem, *, npages, layer):
    b = pl.program_id(0)
    nb = pl.num_programs(0)
    slot = b % 2

    def copies(seq, sl):
        out = []
        for p in range(npages):
            pg = pt_ref[seq * npages + p]
            rows = pl.ds(p * PAGE_SIZE, PAGE_SIZE)
            out.append(pltpu.make_async_copy(lc_ref.at[layer, pg], lbuf.at[sl, rows, :], sem.at[0, sl]))
            out.append(pltpu.make_async_copy(pc_ref.at[layer, pg], pbuf.at[sl, :, rows], sem.at[1, sl]))
            out.append(pltpu.make_async_copy(ic_ref.at[layer, pg], ibuf.at[sl, :, rows], sem.at[2, sl]))
        return out

    @pl.when(b == 0)
    def _():
        for cp in copies(0, 0):
            cp.start()

    @pl.when(b + 1 < nb)
    def _():
        for cp in copies(b + 1, 1 - slot):
            cp.start()

    for cp in copies(b, slot):
        cp.wait()

    qa = qa_ref[0].astype(BF16)
    qp = qp_ref[0]
    lat = lbuf[slot]
    sc = (_dot_nt(qa, lat) + _dot(qp, pbuf[slot])) * MLA_SCALE * ibuf[slot]
    latn = latn_ref[0]
    rnd = lambda a: a.astype(BF16).astype(F32)
    s_self = (jnp.sum(rnd(qa) * rnd(latn), axis=1, keepdims=True)
              + jnp.sum(rnd(qp) * rnd(kpen_ref[0]), axis=1, keepdims=True))
    s_self = s_self * MLA_SCALE * kinvn_ref[0]
    m = jnp.maximum(jnp.max(sc, axis=1, keepdims=True), s_self)
    e = jnp.exp(sc - m)
    e_self = jnp.exp(s_self - m)
    den = jnp.sum(e, axis=1, keepdims=True) + e_self
    ctx = _dot(e, lat) + e_self * latn
    o_ref[0] = ctx / den


def _mla_decode(qa, qp, lat_new, kpe_new, kinv_new, lat_pool, kpe_pool, kinv_pool, layer, page_table):
    nb = qa.shape[0]
    npages = page_table.shape[1]
    s = npages * PAGE_SIZE
    per_seq = lambda a: pl.BlockSpec((1,) + a.shape[1:], lambda b, pt: (b, 0, 0))
    grid_spec = pltpu.PrefetchScalarGridSpec(
        num_scalar_prefetch=1,
        grid=(nb,),
        in_specs=[per_seq(qa), per_seq(qp), per_seq(lat_new), per_seq(kpe_new), per_seq(kinv_new)]
        + [pl.BlockSpec(memory_space=pl.ANY)] * 3,
        out_specs=pl.BlockSpec((1, MLA_HEADS, MLA_KV_RANK), lambda b, pt: (b, 0, 0)),
        scratch_shapes=[
            pltpu.VMEM((2, s, MLA_KV_RANK), F32),
            pltpu.VMEM((2, MLA_ROPE, s), F32),
            pltpu.VMEM((2, MLA_HEADS, s), F32),
            pltpu.SemaphoreType.DMA((3, 2)),
        ],
    )
    return pl.pallas_call(
        functools.partial(_mla_decode_kernel, npages=npages, layer=layer),
        grid_spec=grid_spec,
        out_shape=jax.ShapeDtypeStruct((nb, MLA_HEADS, MLA_KV_RANK), F32),
        compiler_params=_cparams("arbitrary"),
        name="mla_decode",
    )(page_table.reshape(-1), qa, qp, lat_new, kpe_new, kinv_new, lat_pool, kpe_pool, kinv_pool)


def _mla_ctx_out_kernel(ctx_ref, wuv_ref, o_ref):
    for hh in range(MLA_HEADS):
        o_ref[:, hh * MLA_V:(hh + 1) * MLA_V] = _dot(ctx_ref[hh].astype(BF16), wuv_ref[hh]).astype(BF16)


def _mla_ctx_out(ctx_h, wuvh):
    m = ctx_h.shape[1]
    return pl.pallas_call(
        _mla_ctx_out_kernel,
        out_shape=jax.ShapeDtypeStruct((m, MLA_HEADS * MLA_V), BF16),
        name="mla_ctx_out",
    )(ctx_h, wuvh)


def kernel(x_prompt, x_sample, cache_sb_k, cache_sb_v, cache_mla_latent, cache_mla_kpe, cache_mla_kinv,
           state_s5_re, state_s5_im, page_table, ln_mix, ln_mlp, w_up, w_down,
           sb_w_qkv, sb_q_gain, sb_k_gain, sb_w_o,
           s5_a_re, s5_a_im, s5_log_dt, s5_b_re, s5_b_im, s5_c_re, s5_c_im, s5_d, s5_w_glu,
           mla_w_dq, mla_q_ln, mla_w_uq, mla_w_dkv, mla_kv_ln, mla_w_uk, mla_w_uv, mla_q_gain, mla_k_gain,
           mla_w_o):
    nb_p, t, d = x_prompt.shape
    nb_s, t_s, _ = x_sample.shape
    assert t_s == 1 and d == D_MODEL
    npages = page_table.shape[1]
    past_len = npages * PAGE_SIZE
    pos_p = jnp.arange(t, dtype=jnp.int32)
    pos_s = past_len + jnp.arange(t_s, dtype=jnp.int32)
    n_pool = cache_sb_k.shape[1]

    xp = x_prompt.reshape(nb_p * t, d)
    xs = x_sample.reshape(nb_s, d)
    outs = {k: [] for k in ("sbk_p", "sbv_p", "sbk_s", "sbv_s", "lat_p", "kpe_p", "kinv_p",
                            "lat_s", "kpe_s", "kinv_s", "s5r_p", "s5i_p", "s5r_s", "s5i_s")}
    sb_kc = jnp.transpose(cache_sb_k, (0, 1, 3, 4, 2)).reshape(-1, n_pool, SB_NKV, PAGE_SIZE)
    sb_vc = jnp.transpose(cache_sb_v, (0, 1, 3, 4, 2)).reshape(-1, n_pool, SB_NKV, PAGE_SIZE)
    mla_pc = jnp.transpose(cache_mla_kpe, (0, 1, 3, 2))
    mla_ic = jnp.transpose(cache_mla_kinv, (0, 1, 3, 2))
    sb_self_pos = jnp.concatenate([pos_s[-1:], pos_s[-1:]])
    kv_of_head = (jnp.arange(SB_HEADS)[:, None] // SB_GROUP == jnp.arange(SB_KV_HEADS)[None, :])

    for i in range(DEPTH):
        kind, j = i % N_MIXERS, i // N_MIXERS
        g_mix = ln_mix[i][None, :]
        if kind == 0:
            w = sb_w_qkv[j].astype(BF16)
            wo = sb_w_o[j].astype(BF16)
            q, _, _, kt, vt = _sb_proj(xp, g_mix, w, sb_q_gain[j], sb_k_gain[j], nb_p)
            outs["sbk_p"].append(jnp.transpose(kt.reshape(nb_p, SB_KV_HEADS, SB_HEAD_DIM, t), (0, 3, 1, 2)))
            outs["sbv_p"].append(jnp.transpose(vt.reshape(nb_p, SB_KV_HEADS, SB_HEAD_DIM, t), (0, 3, 1, 2)))
            o = _sb_attn(q.reshape(nb_p, t, SB_NQ), kt, vt)
            xp = _proj_res(xp, o.reshape(nb_p * t, SB_NQ), wo)

            q, k, v, _, _ = _sb_proj(xs, g_mix, w, sb_q_gain[j], sb_k_gain[j], 1)
            outs["sbk_s"].append(k.reshape(nb_s, 1, SB_KV_HEADS, SB_HEAD_DIM))
            outs["sbv_s"].append(v.reshape(nb_s, 1, SB_KV_HEADS, SB_HEAD_DIM))
            qbd = (q.reshape(nb_s, SB_HEADS, 1, SB_HEAD_DIM) * kv_of_head[None, :, :, None].astype(BF16))
            qbd = qbd.reshape(nb_s, SB_HEADS, SB_NKV)
            ot = _sb_decode(qbd, k, v, sb_kc, sb_vc, j, page_table, sb_self_pos)
            ot = ot.reshape(nb_s, SB_KV_HEADS, SB_HEAD_DIM, SB_KV_HEADS, SB_GROUP)
            o = jnp.einsum('bkdkg->bkgd', ot).reshape(nb_s, SB_NQ)
            xs = _proj_res(xs, o.astype(BF16), wo)
        elif kind == 1:
            abr, abi, bbr, bbi = _s5_disc(s5_a_re[j], s5_a_im[j], s5_log_dt[j], s5_b_re[j], s5_b_im[j])
            wts = _s5_block_weights(bbr, bbi, s5_c_re[j], s5_c_im[j])
            abr, abi = abr.reshape(1, S5_DIM), abi.reshape(1, S5_DIM)
            dd = s5_d[j][None, :]
            wglu = s5_w_glu[j].astype(BF16)
            zeros = jnp.zeros((nb_p, 1, S5_DIM), F32)
            y, sr, si = _s5(xp.reshape(nb_p, t, d), g_mix, wts, abr, abi, zeros, zeros, dd, wglu, True)
            xp = y.reshape(nb_p * t, d)
            outs["s5r_p"].append(sr.reshape(nb_p, S5_GROUPS, S5_STATE))
            outs["s5i_p"].append(si.reshape(nb_p, S5_GROUPS, S5_STATE))
            h0r = state_s5_re[j].reshape(1, nb_s, S5_DIM)
            h0i = state_s5_im[j].reshape(1, nb_s, S5_DIM)
            y, sr, si = _s5(xs.reshape(1, nb_s, d), g_mix, wts, abr, abi, h0r, h0i, dd, wglu, False)
            xs = y.reshape(nb_s, d)
            outs["s5r_s"].append(sr.reshape(nb_s, S5_GROUPS, S5_STATE))
            outs["s5i_s"].append(si.reshape(nb_s, S5_GROUPS, S5_STATE))
        else:
            mw = _mla_weights(mla_w_dq[j], mla_q_ln[j], mla_w_uq[j], mla_w_dkv[j], mla_kv_ln[j],
                              mla_w_uk[j], mla_w_uv[j], mla_q_gain[j], mla_k_gain[j])
            wo = mla_w_o[j].astype(BF16)
            pe = slice(MLA_NOPE, MLA_QK)
            q, lat, kpet, kinvt, kt, v = _mla_proj(xp, g_mix, mw, pos_p, nb_p)
            outs["lat_p"].append(lat.reshape(nb_p, t, MLA_KV_RANK))
            outs["kpe_p"].append(jnp.transpose(kpet[:, pe, :], (0, 2, 1)))
            outs["kinv_p"].append(jnp.transpose(kinvt, (0, 2, 1)))
            o = _mla_attn(q.reshape(nb_p, t, -1), kt, v.reshape(nb_p, t, -1))
            xp = _proj_res(xp, o.reshape(nb_p * t, -1), wo)

            q, lat, kpet, kinvt, _, _ = _mla_proj(xs, g_mix, mw, jnp.broadcast_to(pos_s, (nb_s,)), 1)
            kpe_s = jnp.transpose(kpet[0, pe, :], (1, 0))
            kinv_s = jnp.transpose(kinvt[0], (1, 0))
            outs["lat_s"].append(lat.reshape(nb_s, 1, MLA_KV_RANK))
            outs["kpe_s"].append(kpe_s.reshape(nb_s, 1, MLA_ROPE))
            outs["kinv_s"].append(kinv_s.reshape(nb_s, 1, MLA_HEADS))
            qa = jnp.transpose(_mla_qabs(q, mw["wukt"]), (1, 0, 2))
            qp = q.reshape(nb_s, MLA_HEADS, MLA_SEG)[:, :, pe]
            ctx = _mla_decode(qa, qp, lat[:, None, :], kpe_s[:, None, :], kinv_s[:, :, None],
                              cache_mla_latent, mla_pc, mla_ic, j, page_table)
            o = _mla_ctx_out(jnp.transpose(ctx, (1, 0, 2)), mw["wuvh"])
            xs = _proj_res(xs, o, wo)
        g_mlp = ln_mlp[i][None, :]
        wu = w_up[i].astype(BF16)
        wd = w_down[i].astype(BF16)
        xp = _mlp(xp, g_mlp, wu, wd)
        xs = _mlp(xs, g_mlp, wu, wd)

    st = lambda k: jnp.stack(outs[k])
    return (xp.reshape(nb_p, t, d), xs.reshape(nb_s, t_s, d),
            st("sbk_p"), st("sbv_p"), st("sbk_s"), st("sbv_s"),
            st("lat_p"), st("kpe_p"), st("kinv_p"), st("lat_s"), st("kpe_s"), st("kinv_s"),
            st("s5r_p"), st("s5i_p"), st("s5r_s"), st("s5i_s"))
```
